```python
import math
import jax, jax.numpy as jnp
from jax import lax
import numpy as np

D_MODEL = 1024
BATCH = 2
SEQ = 8192
DEPTH = 4
DEC_BATCH = 32
DEC_SEQ = 1
PAST_LEN = 8192
PAGE_SIZE = 128

N_MIXERS = 2
N_RET_LAYERS = (DEPTH + 1) // 2
N_DIFF_LAYERS = DEPTH // 2

RET_HEADS = D_MODEL // 256
RET_DK = D_MODEL // RET_HEADS
RET_DV = 2 * RET_DK
RET_CHUNK = 128
RET_ROPE_BASE = 10000.0

DIFF_D = 64
DIFF_HEADS = D_MODEL // (2 * DIFF_D)
DIFF_ROT = DIFF_D // 4
ROPE_THETA = 500000.0
Q_BLOCK = 128
LAMBDA_STD = 0.1

FFN_HIDDEN = -(-8 * D_MODEL // (3 * 256)) * 256

ALPHA = (2.0 * DEPTH) ** 0.25
BETA = (8.0 * DEPTH) ** -0.25
LN_EPS = 1e-5

kernel_name = 'hybrid_retnet_diffattn_step'


def layer_norm(x, g, b):
    xf = x.astype(jnp.float32)
    mu = xf.mean(-1, keepdims=True)
    var = jnp.square(xf - mu).mean(-1, keepdims=True)
    return ((xf - mu) * lax.rsqrt(var + LN_EPS) * g + b).astype(x.dtype)


def post_norm(x, sub, g, b):
    return layer_norm(ALPHA * x + sub, g, b)


def swiglu(x, w_in, w_down):
    gate, up = jnp.split(x @ w_in, 2, axis=-1)
    return (jax.nn.silu(gate) * up) @ w_down


def ret_rotary(x, pos):
    inv = 1.0 / (RET_ROPE_BASE ** jnp.linspace(0.0, 1.0, RET_DK // 2, dtype=jnp.float32))
    ang = pos.astype(jnp.float32)[:, None] * inv[None, :]
    cos, sin = jnp.cos(ang)[:, None, :], jnp.sin(ang)[:, None, :]
    xf = x.astype(jnp.float32)
    x1, x2 = xf[..., 0::2], xf[..., 1::2]
    return jnp.stack([x1 * cos - x2 * sin, x2 * cos + x1 * sin], axis=-1).reshape(x.shape)


def partial_rope(x, pos):
    half = DIFF_ROT // 2
    inv = 1.0 / (ROPE_THETA ** (jnp.arange(half, dtype=jnp.float32) * 2.0 / DIFF_ROT))
    ang = pos.astype(jnp.float32)[:, None] * inv[None, :]
    cos, sin = jnp.cos(ang)[:, None, :], jnp.sin(ang)[:, None, :]
    xr = x[..., :DIFF_ROT].astype(jnp.float32)
    x1, x2 = xr[..., :half], xr[..., half:]
    rot = jnp.concatenate([x1 * cos - x2 * sin, x2 * cos + x1 * sin], axis=-1)
    return jnp.concatenate([rot.astype(x.dtype), x[..., DIFF_ROT:]], axis=-1)


def ret_log_gamma():
    return jnp.log(1.0 - 2.0 ** (-5.0 - jnp.arange(RET_HEADS, dtype=jnp.float32)))


def retention_chunk(q, k, v, s0, log_gamma):
    C = q.shape[1]
    idx = jnp.arange(C, dtype=jnp.float32)
    rel = idx[:, None] - idx[None, :]
    decay = jnp.where(rel >= 0, jnp.exp(log_gamma[:, None, None] * jnp.maximum(rel, 0.0)), 0.0)
    scores = jnp.einsum('bihd,bjhd->bhij', q, k) * decay[None]
    inner = jnp.einsum('bhij,bjhe->bihe', scores, v)
    q_decay = jnp.exp(log_gamma[None, :] * (idx[:, None] + 1.0))
    cross = jnp.einsum('bihd,bhde->bihe', q, s0) * q_decay[None, :, :, None]
    k_decay = jnp.exp(log_gamma[None, :] * (C - 1.0 - idx[:, None]))
    s_new = (jnp.exp(log_gamma * C)[None, :, None, None] * s0
             + jnp.einsum('bjhd,bjhe->bhde', k * k_decay[None, :, :, None], v))
    return inner + cross, s_new


def retention_mixer(x, pos, s0, w_in, gn_g, w_o):
    B, T, _ = x.shape
    hk, hv = RET_HEADS * RET_DK, RET_HEADS * RET_DV
    proj = x @ w_in
    q = ret_rotary(proj[..., :hk].reshape(B, T, RET_HEADS, RET_DK), pos)
    k = ret_rotary(proj[..., hk:2 * hk].reshape(B, T, RET_HEADS, RET_DK), pos) * (RET_DK ** -0.5)
    v = proj[..., 2 * hk:2 * hk + hv].reshape(B, T, RET_HEADS, RET_DV).astype(jnp.float32)
    g = proj[..., 2 * hk + hv:].astype(jnp.float32)
    s0 = s0.astype(jnp.float32)
    log_gamma = ret_log_gamma()
    if T % RET_CHUNK == 0 and T > RET_CHUNK:
        n = T // RET_CHUNK

        def to_chunks(a):
            return a.reshape(B, n, RET_CHUNK, *a.shape[2:]).swapaxes(0, 1)

        def step(s, qkv):
            o_c, s_c = retention_chunk(qkv[0], qkv[1], qkv[2], s, log_gamma)
            return s_c, o_c

        s_new, o = lax.scan(step, s0, (to_chunks(q), to_chunks(k), to_chunks(v)))
        o = o.swapaxes(0, 1).reshape(B, T, RET_HEADS, RET_DV)
    else:
        o, s_new = retention_chunk(q, k, v, s0, log_gamma)
    mu = o.mean(-1, keepdims=True)
    var = jnp.square(o - mu).mean(-1, keepdims=True)
    o = ((o - mu) * lax.rsqrt(var + LN_EPS)).reshape(B, T, hv) * gn_g
    out = (jax.nn.silu(g) * o).astype(x.dtype) @ w_o
    return out, s_new.astype(x.dtype)


def diff_lambda(lq1, lk1, lq2, lk2, lam_init):
    return (jnp.exp(jnp.sum(lq1.astype(jnp.float32) * lk1.astype(jnp.float32)))
            - jnp.exp(jnp.sum(lq2.astype(jnp.float32) * lk2.astype(jnp.float32))) + lam_init)


def diff_project(x, pos, w_in):
    B, T, _ = x.shape
    hd = DIFF_HEADS * 2 * DIFF_D
    proj = x @ w_in
    q = partial_rope(proj[..., :hd].reshape(B, T, 2 * DIFF_HEADS, DIFF_D), pos) * (DIFF_D ** -0.5)
    k = partial_rope(proj[..., hd:2 * hd].reshape(B, T, 2 * DIFF_HEADS, DIFF_D), pos)
    v = proj[..., 2 * hd:].reshape(B, T, DIFF_HEADS, 2 * DIFF_D)
    return (q.reshape(B, T, DIFF_HEADS, 2, DIFF_D), k.reshape(B, T, DIFF_HEADS, 2, DIFF_D), v)


def diff_attend(q, q_pos, k, v, lam):
    s = jnp.einsum('bqhcd,bkhcd->bhcqk', q, k, preferred_element_type=jnp.float32)
    k_pos = jnp.arange(k.shape[1])
    s = jnp.where((k_pos[None, :] <= q_pos[:, None])[None, None, None], s, -jnp.inf)
    p = jax.nn.softmax(s, axis=-1)
    a = p[:, :, 0] - lam * p[:, :, 1]
    return jnp.einsum('bhqk,bkhe->bqhe', a.astype(v.dtype), v)


def diff_prompt(q, k, v, lam):
    B, T = q.shape[0], q.shape[1]
    nb = T // Q_BLOCK
    qb = q.reshape(B, nb, Q_BLOCK, DIFF_HEADS, 2, DIFF_D).swapaxes(0, 1)

    def block(args):
        q_i, start = args
        return diff_attend(q_i, start + jnp.arange(Q_BLOCK), k, v, lam)

    out = lax.map(block, (qb, jnp.arange(nb) * Q_BLOCK))
    return out.swapaxes(0, 1).reshape(B, T, DIFF_HEADS, 2 * DIFF_D)


def diff_output(o, lam_init, subln_g, w_o, dtype):
    B, T = o.shape[0], o.shape[1]
    of = o.astype(jnp.float32)
    of = of * lax.rsqrt(jnp.square(of).mean(-1, keepdims=True) + LN_EPS)
    of = of.reshape(B, T, DIFF_HEADS * 2 * DIFF_D) * subln_g * (1.0 - lam_init)
    return of.astype(dtype) @ w_o


def setup_inputs(seed: int = 0) -> dict:
    key = jax.random.key(seed)
    ks = jax.random.split(key, 24)
    f32 = jnp.float32
    n_pages = PAST_LEN // PAGE_SIZE
    n_phys = (5 * DEC_BATCH * n_pages + 3) // 4
    hk, hv = RET_HEADS * RET_DK, RET_HEADS * RET_DV
    hd = DIFF_HEADS * 2 * DIFF_D

    def nrm(k, shape, s):
        return jax.random.normal(k, shape, f32) * s

    ret_col = jnp.concatenate([jnp.ones((2 * hk,), f32), jnp.full((hv,), BETA, f32), jnp.ones((hv,), f32)])
    diff_col = jnp.concatenate([jnp.ones((2 * hd,), f32), jnp.full((hd,), BETA, f32)])
    page_table = jax.random.permutation(ks[5], n_phys)[:DEC_BATCH * n_pages].reshape(DEC_BATCH, n_pages).astype(jnp.int32)
    return {
        'x_prompt': nrm(ks[0], (BATCH, SEQ, D_MODEL), 1.0),
        'x_sample': nrm(ks[1], (DEC_BATCH, DEC_SEQ, D_MODEL), 1.0),
        'state_ret': nrm(ks[2], (N_RET_LAYERS, DEC_BATCH, RET_HEADS, RET_DK, RET_DV), 0.5),
        'cache_k': nrm(ks[3], (N_DIFF_LAYERS, n_phys, PAGE_SIZE, DIFF_HEADS, 2 * DIFF_D), 1.0),
        'cache_v': nrm(ks[4], (N_DIFF_LAYERS, n_phys, PAGE_SIZE, DIFF_HEADS, 2 * DIFF_D), 1.0),
        'page_table': page_table,
        'ret_w_in': nrm(ks[6], (N_RET_LAYERS, D_MODEL, 2 * hk + 2 * hv), D_MODEL ** -0.5) * ret_col,
        'ret_gn_g': 1.0 + nrm(ks[7], (N_RET_LAYERS, hv), 0.02),
        'ret_w_o': nrm(ks[8], (N_RET_LAYERS, hv, D_MODEL), BETA * hv ** -0.5),
        'diff_w_in': nrm(ks[9], (N_DIFF_LAYERS, D_MODEL, 3 * hd), D_MODEL ** -0.5) * diff_col,
        'diff_lq1': nrm(ks[10], (N_DIFF_LAYERS, DIFF_D), LAMBDA_STD),
        'diff_lk1': nrm(ks[11], (N_DIFF_LAYERS, DIFF_D), LAMBDA_STD),
        'diff_lq2': nrm(ks[12], (N_DIFF_LAYERS, DIFF_D), LAMBDA_STD),
        'diff_lk2': nrm(ks[13], (N_DIFF_LAYERS, DIFF_D), LAMBDA_STD),
        'diff_subln_g': 1.0 + nrm(ks[14], (N_DIFF_LAYERS, hd), 0.02),
        'diff_w_o': nrm(ks[15], (N_DIFF_LAYERS, hd, D_MODEL), BETA * hd ** -0.5),
        'ffn_w_in': nrm(ks[16], (DEPTH, D_MODEL, 2 * FFN_HIDDEN), D_MODEL ** -0.5),
        'ffn_w_down': nrm(ks[17], (DEPTH, FFN_HIDDEN, D_MODEL), BETA * FFN_HIDDEN ** -0.5),
        'ln1_g': 1.0 + nrm(ks[18], (DEPTH, D_MODEL), 0.02),
        'ln1_b': nrm(ks[19], (DEPTH, D_MODEL), 0.02),
        'ln2_g': 1.0 + nrm(ks[20], (DEPTH, D_MODEL), 0.02),
        'ln2_b': nrm(ks[21], (DEPTH, D_MODEL), 0.02),
    }


def reference(x_prompt, x_sample, state_ret, cache_k, cache_v, page_table,
              ret_w_in, ret_gn_g, ret_w_o,
              diff_w_in, diff_lq1, diff_lk1, diff_lq2, diff_lk2, diff_subln_g, diff_w_o,
              ffn_w_in, ffn_w_down, ln1_g, ln1_b, ln2_g, ln2_b):
    bp, tp = x_prompt.shape[0], x_prompt.shape[1]
    bs, ts = x_sample.shape[0], x_sample.shape[1]
    past_len = page_table.shape[1] * PAGE_SIZE
    pos_p = jnp.arange(tp)
    pos_s = past_len + jnp.arange(ts)
    xp, xs = x_prompt, x_sample
    ret_p, ret_s, kc_p, vc_p, kc_s, vc_s = [], [], [], [], [], []
    for i in range(DEPTH):
        j = i // N_MIXERS
        if i % N_MIXERS == 0:
            s_zero = jnp.zeros((bp, RET_HEADS, RET_DK, RET_DV), jnp.float32)
            mp, sp = retention_mixer(xp, pos_p, s_zero, ret_w_in[j], ret_gn_g[j], ret_w_o[j])
            ms, ss = retention_mixer(xs, pos_s, state_ret[j], ret_w_in[j], ret_gn_g[j], ret_w_o[j])
            ret_p.append(sp)
            ret_s.append(ss)
        else:
            lam_init = 0.8 - 0.6 * math.exp(-0.3 * (i + 1))
            lam = diff_lambda(diff_lq1[j], diff_lk1[j], diff_lq2[j], diff_lk2[j], lam_init)
            qp, kp, vp = diff_project(xp, pos_p, diff_w_in[j])
            mp = diff_output(diff_prompt(qp, kp, vp, lam), lam_init, diff_subln_g[j], diff_w_o[j], xp.dtype)
            qs, kn, vn = diff_project(xs, pos_s, diff_w_in[j])
            k_past = cache_k[j][page_table].reshape(bs, past_len, DIFF_HEADS, 2, DIFF_D).astype(kn.dtype)
            v_past = cache_v[j][page_table].reshape(bs, past_len, DIFF_HEADS, 2 * DIFF_D).astype(vn.dtype)
            k_all = jnp.concatenate([k_past, kn], axis=1)
            v_all = jnp.concatenate([v_past, vn], axis=1)
            ms = diff_output(diff_attend(qs, pos_s, k_all, v_all, lam), lam_init, diff_subln_g[j], diff_w_o[j], xs.dtype)
            kc_p.append(kp.reshape(bp, tp, DIFF_HEADS, 2 * DIFF_D))
            vc_p.append(vp)
            kc_s.append(kn.reshape(bs, ts, DIFF_HEADS, 2 * DIFF_D))
            vc_s.append(vn)
        xp = post_norm(xp, mp, ln1_g[i], ln1_b[i])
        xs = post_norm(xs, ms, ln1_g[i], ln1_b[i])
        xp = post_norm(xp, swiglu(xp, ffn_w_in[i], ffn_w_down[i]), ln2_g[i], ln2_b[i])
        xs = post_norm(xs, swiglu(xs, ffn_w_in[i], ffn_w_down[i]), ln2_g[i], ln2_b[i])
    return (xp, xs, jnp.stack(ret_p), jnp.stack(kc_p), jnp.stack(vc_p),
            jnp.stack(ret_s), jnp.stack(kc_s), jnp.stack(vc_s))
```

```python
import functools
import math

import numpy as np
import jax
import jax.numpy as jnp
from jax import lax
from jax.experimental import pallas as pl
from jax.experimental.pallas import tpu as pltpu

BF = jnp.bfloat16
F32 = jnp.float32

RET_DK = 256
RET_DV = 512
RET_ROPE_BASE = 10000.0
DIFF_D = 64
DIFF_ROT = DIFF_D // 4
ROPE_THETA = 500000.0
PAGE_SIZE = 128
LN_EPS = 1e-5
N_MIXERS = 2

LANES = 128
VMEM_LIMIT = 48 * 1024 * 1024


def _cparams(sem):
    return pltpu.CompilerParams(dimension_semantics=sem, vmem_limit_bytes=VMEM_LIMIT)


def _sigmoid(x):
    return 1.0 / (1.0 + jnp.exp(-x))


def _layer_norm(y, g, b):
    mu = jnp.mean(y, axis=-1, keepdims=True)
    d = y - mu
    var = jnp.mean(d * d, axis=-1, keepdims=True)
    return d * lax.rsqrt(var + LN_EPS) * g + b


def _row_tile(m, pref):
    return pref if m % pref == 0 else m


def _proj_plain_kernel(x_ref, w_ref, *o_refs):
    acc = jnp.dot(x_ref[...].astype(BF), w_ref[...], preferred_element_type=F32)
    for o_ref in o_refs:
        o_ref[...] = acc.astype(o_ref.dtype)


def _proj_plain(x, w, out_dtypes, *, tm, tn, name):
    m, k = x.shape
    n = w.shape[1]
    return pl.pallas_call(
        _proj_plain_kernel,
        grid=(m // tm, n // tn),
        in_specs=[pl.BlockSpec((tm, k), lambda i, j: (i, 0)),
                  pl.BlockSpec((k, tn), lambda i, j: (0, j))],
        out_specs=[pl.BlockSpec((tm, tn), lambda i, j: (i, j)) for _ in out_dtypes],
        out_shape=[jax.ShapeDtypeStruct((m, n), dt) for dt in out_dtypes],
        compiler_params=_cparams(("parallel", "parallel")),
        name=name,
    )(x, w)


def _ret_qk_kernel(x_ref, w_ref, cos_ref, sin_ref, o_ref, *, n_q_blocks, k_scale):
    acc = jnp.dot(x_ref[...].astype(BF), w_ref[...], preferred_element_type=F32)
    half = RET_DK // 2
    c = cos_ref[...]
    s = sin_ref[...]
    x1 = acc[:, :half]
    x2 = acc[:, half:]
    scale = jnp.where(pl.program_id(1) >= n_q_blocks, k_scale, 1.0).astype(F32)
    o_ref[:, :half] = ((x1 * c - x2 * s) * scale).astype(o_ref.dtype)
    o_ref[:, half:] = ((x2 * c + x1 * s) * scale).astype(o_ref.dtype)


def _ret_qk_proj(x, w_qk, cos, sin, out_dtype, *, tm, name):
    m, k = x.shape
    n = w_qk.shape[1]
    nb = n // RET_DK
    ntab = cos.shape[0] // tm
    kern = functools.partial(_ret_qk_kernel, n_q_blocks=nb // 2, k_scale=RET_DK ** -0.5)
    return pl.pallas_call(
        kern,
        grid=(m // tm, nb),
        in_specs=[pl.BlockSpec((tm, k), lambda i, j: (i, 0)),
                  pl.BlockSpec((k, RET_DK), lambda i, j: (0, j)),
                  pl.BlockSpec((tm, RET_DK // 2), lambda i, j: (i % ntab, 0)),
                  pl.BlockSpec((tm, RET_DK // 2), lambda i, j: (i % ntab, 0))],
        out_specs=pl.BlockSpec((tm, RET_DK), lambda i, j: (i, j)),
        out_shape=jax.ShapeDtypeStruct((m, n), out_dtype),
        compiler_params=_cparams(("parallel", "parallel")),
        name=name,
    )(x, w_qk, cos, sin)


def _diff_rope_kernel(x_ref, w_ref, c_ref, a_ref, b_ref, *o_refs, scale):
    acc = jnp.dot(x_ref[...].astype(BF), w_ref[...], preferred_element_type=F32)
    c = c_ref[...]
    a = a_ref[...]
    b = b_ref[...]
    half = DIFF_ROT // 2
    for t in range(acc.shape[1] // LANES):
        xs = acc[:, t * LANES:(t + 1) * LANES]
        up = pltpu.roll(xs, LANES - half, axis=1)
        dn = pltpu.roll(xs, half, axis=1)
        r = xs * c + up * a + dn * b
        if scale != 1.0:
            r = r * scale
        for o_ref in o_refs:
            o_ref[:, t * LANES:(t + 1) * LANES] = r.astype(o_ref.dtype)


def _diff_rope_proj(x, w, tabs, out_dtypes, *, scale, tm, tn, name):
    m, k = x.shape
    n = w.shape[1]
    ntab = tabs[0].shape[0] // tm
    kern = functools.partial(_diff_rope_kernel, scale=scale)
    tab_spec = pl.BlockSpec((tm, LANES), lambda i, j: (i % ntab, 0))
    return pl.pallas_call(
        kern,
        grid=(m // tm, n // tn),
        in_specs=[pl.BlockSpec((tm, k), lambda i, j: (i, 0)),
                  pl.BlockSpec((k, tn), lambda i, j: (0, j)),
                  tab_spec, tab_spec, tab_spec],
        out_specs=[pl.BlockSpec((tm, tn), lambda i, j: (i, j)) for _ in out_dtypes],
        out_shape=[jax.ShapeDtypeStruct((m, n), dt) for dt in out_dtypes],
        compiler_params=_cparams(("parallel", "parallel")),
        name=name,
    )(x, w, *tabs)


def _out_norm_kernel(a_ref, w_ref, x_ref, g_ref, b_ref, o_ref, ob_ref, *, alpha):
    sub = jnp.dot(a_ref[...].astype(BF), w_ref[...], preferred_element_type=F32)
    y = _layer_norm(alpha * x_ref[...] + sub, g_ref[...], b_ref[...])
    o_ref[...] = y
    ob_ref[...] = y.astype(BF)


def _out_norm(a, w, x, g, b, *, alpha, tm, name):
    m, ka = a.shape
    d = w.shape[1]
    kern = functools.partial(_out_norm_kernel, alpha=alpha)
    return pl.pallas_call(
        kern,
        grid=(m // tm,),
        in_specs=[pl.BlockSpec((tm, ka), lambda i: (i, 0)),
                  pl.BlockSpec((ka, d), lambda i: (0, 0)),
                  pl.BlockSpec((tm, d), lambda i: (i, 0)),
                  pl.BlockSpec((1, d), lambda i: (0, 0)),
                  pl.BlockSpec((1, d), lambda i: (0, 0))],
        out_specs=[pl.BlockSpec((tm, d), lambda i: (i, 0)),
                   pl.BlockSpec((tm, d), lambda i: (i, 0))],
        out_shape=[jax.ShapeDtypeStruct((m, d), F32), jax.ShapeDtypeStruct((m, d), BF)],
        compiler_params=_cparams(("parallel",)),
        name=name,
    )(a, w, x, g, b)


def _ffn_kernel(x_ref, xb_ref, wg_ref, wu_ref, wd_ref, g_ref, b_ref, o_ref, ob_ref, acc_ref, *, alpha):
    j = pl.program_id(1)

    @pl.when(j == 0)
    def _():
        acc_ref[...] = jnp.zeros_like(acc_ref)

    xb = xb_ref[...]
    gate = jnp.dot(xb, wg_ref[...], preferred_element_type=F32)
    up = jnp.dot(xb, wu_ref[...], preferred_element_type=F32)
    h = gate * _sigmoid(gate) * up
    acc_ref[...] += jnp.dot(h.astype(BF), wd_ref[...], preferred_element_type=F32)

    @pl.when(j == pl.num_programs(1) - 1)
    def _():
        y = _layer_norm(alpha * x_ref[...] + acc_ref[...], g_ref[...], b_ref[...])
        o_ref[...] = y
        ob_ref[...] = y.astype(BF)


def _ffn(x, xb, w_in, w_down, g, b, *, alpha, tm, th, name):
    m, d = x.shape
    hidden = w_down.shape[0]
    nh = hidden // th
    kern = functools.partial(_ffn_kernel, alpha=alpha)
    return pl.pallas_call(
        kern,
        grid=(m // tm, nh),
        in_specs=[pl.BlockSpec((tm, d), lambda i, j: (i, 0)),
                  pl.BlockSpec((tm, d), lambda i, j: (i, 0)),
                  pl.BlockSpec((d, th), lambda i, j: (0, j)),
                  pl.BlockSpec((d, th), lambda i, j: (0, nh + j)),
                  pl.BlockSpec((th, d), lambda i, j: (j, 0)),
                  pl.BlockSpec((1, d), lambda i, j: (0, 0)),
                  pl.BlockSpec((1, d), lambda i, j: (0, 0))],
        out_specs=[pl.BlockSpec((tm, d), lambda i, j: (i, 0)),
                   pl.BlockSpec((tm, d), lambda i, j: (i, 0))],
        out_shape=[jax.ShapeDtypeStruct((m, d), F32), jax.ShapeDtypeStruct((m, d), BF)],
        scratch_shapes=[pltpu.VMEM((tm, d), F32)],
        compiler_params=_cparams(("parallel", "arbitrary")),
        name=name,
    )(x, xb, w_in, w_in, w_down, g, b)


def _ret_chunk_kernel(lg_ref, q_ref, k_ref, v_ref, g_ref, gn_ref, o_ref, st_ref, s_ref, *, chunk):
    c = pl.program_id(2)
    half = RET_DK // 2

    @pl.when(c == 0)
    def _():
        s_ref[...] = jnp.zeros_like(s_ref)

    lg_row = lg_ref[0]
    lg = lg_row[:, :1]
    ii = lax.broadcasted_iota(jnp.int32, (chunk, chunk), 0)
    jj = lax.broadcasted_iota(jnp.int32, (chunk, chunk), 1)
    rel = (ii - jj).astype(F32)
    decay = jnp.where(rel >= 0, jnp.exp(lg_row * jnp.maximum(rel, 0.0)), 0.0)
    idx = lax.broadcasted_iota(jnp.int32, (chunk, 1), 0).astype(F32)
    q_decay = jnp.exp(lg * (idx + 1.0))
    k_decay = jnp.exp(lg * (chunk - 1.0 - idx))
    chunk_decay = jnp.exp(lg * float(chunk))

    q = q_ref[...]
    k = k_ref[...]
    v = v_ref[...]
    s_old = s_ref[...]
    scores = lax.dot_general(q, k, (((1,), (1,)), ((), ())), preferred_element_type=F32) * decay
    inner = jnp.dot(scores.astype(BF), v, preferred_element_type=F32)
    cross = jnp.dot(q, s_old.astype(BF), preferred_element_type=F32) * q_decay
    kd_t = (k.astype(F32) * k_decay).T.astype(BF)
    s_ref[...] = chunk_decay * s_old + jnp.dot(kd_t, v, preferred_element_type=F32)

    o = inner + cross
    mu = jnp.mean(o, axis=-1, keepdims=True)
    d = o - mu
    var = jnp.mean(d * d, axis=-1, keepdims=True)
    on = d * lax.rsqrt(var + LN_EPS) * gn_ref[...]
    gt = g_ref[...].astype(F32)
    o_ref[...] = (gt * _sigmoid(gt) * on).astype(o_ref.dtype)

    @pl.when(c == pl.num_programs(2) - 1)
    def _():
        st_ref[0, 0, :, :RET_DV] = s_ref[:half, :]
        st_ref[0, 0, :, RET_DV:] = s_ref[half:, :]


def _ret_chunk(lg_tab, qk, v, g, gn_g, *, batch, heads, chunk, name):
    m = qk.shape[0]
    t = m // batch
    nc = t // chunk
    kern = functools.partial(_ret_chunk_kernel, chunk=chunk)
    return pl.pallas_call(
        kern,
        grid=(batch, heads, nc),
        in_specs=[pl.BlockSpec((1, 1, chunk), lambda b, h, c: (h, 0, 0)),
                  pl.BlockSpec((chunk, RET_DK), lambda b, h, c: (b * nc + c, h)),
                  pl.BlockSpec((chunk, RET_DK), lambda b, h, c: (b * nc + c, heads + h)),
                  pl.BlockSpec((chunk, RET_DV), lambda b, h, c: (b * nc + c, h)),
                  pl.BlockSpec((chunk, RET_DV), lambda b, h, c: (b * nc + c, h)),
                  pl.BlockSpec((1, RET_DV), lambda b, h, c: (0, h))],
        out_specs=[pl.BlockSpec((chunk, RET_DV), lambda b, h, c: (b * nc + c, h)),
                   pl.BlockSpec((1, 1, RET_DK // 2, 2 * RET_DV), lambda b, h, c: (b, h, 0, 0))],
        out_shape=[jax.ShapeDtypeStruct((m, heads * RET_DV), BF),
                   jax.ShapeDtypeStruct((batch, heads, RET_DK // 2, 2 * RET_DV), F32)],
        scratch_shapes=[pltpu.VMEM((RET_DK, RET_DV), F32)],
        compiler_params=_cparams(("parallel", "parallel", "arbitrary")),
        name=name,
    )(lg_tab, qk, qk, v, g, gn_g)


def _ret_step_kernel(lg_ref, q_ref, k_ref, v_ref, g_ref, gn_ref, s0_ref, o_ref, st_ref, *, heads):
    half = RET_DK // 2
    qrow = q_ref[0]
    krow = k_ref[0]
    vrow = v_ref[0]
    grow = g_ref[0]
    for h in range(heads):
        gamma = jnp.exp(lg_ref[h][:, :1])
        qe = qrow[:, h * RET_DK:h * RET_DK + half]
        qo = qrow[:, h * RET_DK + half:(h + 1) * RET_DK]
        ke = krow[:, h * RET_DK:h * RET_DK + half]
        ko = krow[:, h * RET_DK + half:(h + 1) * RET_DK]
        vh = vrow[:, h * RET_DV:(h + 1) * RET_DV]
        s_e = s0_ref[0, h, :, :RET_DV]
        s_o = s0_ref[0, h, :, RET_DV:]
        ke_col = jnp.broadcast_to(ke, (8, half)).T[:, :1]
        ko_col = jnp.broadcast_to(ko, (8, half)).T[:, :1]
        st_ref[0, h, :, :RET_DV] = gamma * s_e + ke_col * vh
        st_ref[0, h, :, RET_DV:] = gamma * s_o + ko_col * vh
        qe8 = jnp.broadcast_to(qe, (8, half)).astype(BF)
        qo8 = jnp.broadcast_to(qo, (8, half)).astype(BF)
        cross = (jnp.dot(qe8, s_e.astype(BF), preferred_element_type=F32)
                 + jnp.dot(qo8, s_o.astype(BF), preferred_element_type=F32))[:1]
        qk = jnp.sum(qe * ke + qo * ko, axis=-1, keepdims=True)
        o = qk * vh + gamma * cross
        mu = jnp.mean(o, axis=-1, keepdims=True)
        d = o - mu
        var = jnp.mean(d * d, axis=-1, keepdims=True)
        on = d * lax.rsqrt(var + LN_EPS) * gn_ref[:, h * RET_DV:(h + 1) * RET_DV]
        gt = grow[:, h * RET_DV:(h + 1) * RET_DV]
        o_ref[0, :, h * RET_DV:(h + 1) * RET_DV] = (gt * _sigmoid(gt) * on).astype(o_ref.dtype)


def _ret_step(lg_tab, qk, v, g, gn_g, s0_pair, *, heads, name):
    bs = qk.shape[0]
    hk = heads * RET_DK
    hv = heads * RET_DV
    q3 = qk.reshape(bs, 1, 2 * hk)
    v3 = v.reshape(bs, 1, hv)
    g3 = g.reshape(bs, 1, hv)
    kern = functools.partial(_ret_step_kernel, heads=heads)
    st_spec = pl.BlockSpec((1, heads, RET_DK // 2, 2 * RET_DV), lambda b: (b, 0, 0, 0))
    out, st = pl.pallas_call(
        kern,
        grid=(bs,),
        in_specs=[pl.BlockSpec(lg_tab.shape, lambda b: (0, 0, 0)),
                  pl.BlockSpec((1, 1, hk), lambda b: (b, 0, 0)),
                  pl.BlockSpec((1, 1, hk), lambda b: (b, 0, 1)),
                  pl.BlockSpec((1, 1, hv), lambda b: (b, 0, 0)),
                  pl.BlockSpec((1, 1, hv), lambda b: (b, 0, 0)),
                  pl.BlockSpec((1, hv), lambda b: (0, 0)),
                  st_spec],
        out_specs=[pl.BlockSpec((1, 1, hv), lambda b: (b, 0, 0)), st_spec],
        out_shape=[jax.ShapeDtypeStruct((bs, 1, hv), BF),
                   jax.ShapeDtypeStruct(s0_pair.shape, F32)],
        compiler_params=_cparams(("parallel",)),
        name=name,
    )(lg_tab, q3, q3, v3, g3, gn_g, s0_pair)
    return out.reshape(bs, hv), st


def _diff_lambda(lq1_ref, lk1_ref, lq2_ref, lk2_ref, lam_init):
    a = jnp.sum(lq1_ref[...] * lk1_ref[...], axis=-1, keepdims=True)
    b = jnp.sum(lq2_ref[...] * lk2_ref[...], axis=-1, keepdims=True)
    return jnp.exp(a) - jnp.exp(b) + lam_init


def _diff_flash_kernel(qi_ref, kj_ref, q_ref, k_ref, v_ref, lq1_ref, lk1_ref, lq2_ref, lk2_ref, sg_ref,
                       o_ref, qs_ref, m_ref, l_ref, acc_ref, *, tq, tk, lam_init):
    step = pl.program_id(2)
    qi = qi_ref[step]
    kj = kj_ref[step]
    last_kj = ((qi + 1) * tq - 1) // tk

    @pl.when(kj == 0)
    def _():
        q = q_ref[...]
        lane = lax.broadcasted_iota(jnp.int32, q.shape, 1)
        zero = jnp.zeros_like(q)
        qs_ref[:tq, :] = jnp.where(lane < DIFF_D, q, zero)
        qs_ref[tq:, :] = jnp.where(lane >= DIFF_D, q, zero)
        m_ref[...] = jnp.full_like(m_ref, -jnp.inf)
        l_ref[...] = jnp.zeros_like(l_ref)
        acc_ref[...] = jnp.zeros_like(acc_ref)

    def update(masked):
        s = lax.dot_general(qs_ref[...], k_ref[...], (((1,), (1,)), ((), ())),
                            preferred_element_type=F32)
        if masked:
            row = lax.broadcasted_iota(jnp.int32, (2 * tq, tk), 0)
            row = jnp.where(row >= tq, row - tq, row) + qi * tq
            col = lax.broadcasted_iota(jnp.int32, (2 * tq, tk), 1) + kj * tk
            s = jnp.where(col <= row, s, -jnp.inf)
        m_prev = m_ref[...]
        m_new = jnp.maximum(m_prev, jnp.max(s, axis=-1, keepdims=True))
        alpha = jnp.exp(m_prev - m_new)
        p = jnp.exp(s - m_new)
        l_ref[...] = alpha * l_ref[...] + jnp.sum(p, axis=-1, keepdims=True)
        acc_ref[...] = alpha * acc_ref[...] + jnp.dot(p.astype(BF), v_ref[...], preferred_element_type=F32)
        m_ref[...] = m_new

    straddles = (kj + 1) * tk - 1 > qi * tq

    @pl.when(straddles)
    def _():
        update(True)

    @pl.when(jnp.logical_not(straddles))
    def _():
        update(False)

    @pl.when(kj == last_kj)
    def _():
        lam = _diff_lambda(lq1_ref, lk1_ref, lq2_ref, lk2_ref, lam_init)
        o1 = acc_ref[:tq, :] / l_ref[:tq, :]
        o2 = acc_ref[tq:, :] / l_ref[tq:, :]
        o = o1 - lam * o2
        o = o * lax.rsqrt(jnp.mean(o * o, axis=-1, keepdims=True) + LN_EPS)
        o_ref[...] = (o * sg_ref[...] * (1.0 - lam_init)).astype(o_ref.dtype)


def _diff_flash(q, k, v, lq1, lk1, lq2, lk2, subln_g, *, batch, heads, tq, tk, lam_init, name):
    m = q.shape[0]
    t = m // batch
    nq, nk = t // tq, t // tk
    qi_list, kj_list = [], []
    for i in range(nq):
        for j in range(((i + 1) * tq - 1) // tk + 1):
            qi_list.append(i)
            kj_list.append(j)
    qi_tab = jnp.asarray(np.array(qi_list, np.int32))
    kj_tab = jnp.asarray(np.array(kj_list, np.int32))
    hd = 2 * DIFF_D
    kern = functools.partial(_diff_flash_kernel, tq=tq, tk=tk, lam_init=lam_init)
    vec_spec = pl.BlockSpec((1, DIFF_D), lambda b, h, s, qi, kj: (0, 0))
    grid_spec = pltpu.PrefetchScalarGridSpec(
        num_scalar_prefetch=2,
        grid=(batch, heads, len(qi_list)),
        in_specs=[pl.BlockSpec((tq, hd), lambda b, h, s, qi, kj: (b * nq + qi[s], h)),
                  pl.BlockSpec((tk, hd), lambda b, h, s, qi, kj: (b * nk + kj[s], h)),
                  pl.BlockSpec((tk, hd), lambda b, h, s, qi, kj: (b * nk + kj[s], h)),
                  vec_spec, vec_spec, vec_spec, vec_spec,
                  pl.BlockSpec((1, hd), lambda b, h, s, qi, kj: (0, h))],
        out_specs=pl.BlockSpec((tq, hd), lambda b, h, s, qi, kj: (b * nq + qi[s], h)),
        scratch_shapes=[pltpu.VMEM((2 * tq, hd), BF),
                        pltpu.VMEM((2 * tq, 1), F32),
                        pltpu.VMEM((2 * tq, 1), F32),
                        pltpu.VMEM((2 * tq, hd), F32)],
    )
    return pl.pallas_call(
        kern,
        grid_spec=grid_spec,
        out_shape=jax.ShapeDtypeStruct((m, heads * hd), BF),
        compiler_params=_cparams(("parallel", "parallel", "arbitrary")),
        name=name,
    )(qi_tab, kj_tab, q, k, v, lq1, lk1, lq2, lk2, subln_g)


def _diff_decode_kernel(pt_ref, q_ref, kn_ref, vn_ref, lq1_ref, lk1_ref, lq2_ref, lk2_ref, sg_ref,
                        k_ref, v_ref, o_ref, m_ref, l_ref, acc_ref, *, heads, lam_init):
    del pt_ref
    p_idx = pl.program_id(1)
    rows = 2 * heads

    @pl.when(p_idx == 0)
    def _():
        m_ref[...] = jnp.full_like(m_ref, -jnp.inf)
        l_ref[...] = jnp.zeros_like(l_ref)
        acc_ref[...] = jnp.zeros_like(acc_ref)

    q8 = q_ref[0]
    lane = lax.broadcasted_iota(jnp.int32, q8.shape, 1)
    q16 = jnp.concatenate([jnp.where(lane < DIFF_D, q8, 0.0),
                           jnp.where(lane >= DIFF_D, q8, 0.0)], axis=0)

    kp = k_ref[0]
    vp = v_ref[0]
    ncol = kp.shape[0]
    s = lax.dot_general(q16.astype(BF), kp.astype(BF), (((1,), (1,)), ((), ())),
                        preferred_element_type=F32)
    r_head = lax.broadcasted_iota(jnp.int32, (rows, ncol), 0) % heads
    c_head = lax.broadcasted_iota(jnp.int32, (rows, ncol), 1) % heads
    s = jnp.where(r_head == c_head, s, -jnp.inf)
    m_prev = m_ref[...]
    m_new = jnp.maximum(m_prev, jnp.max(s, axis=-1, keepdims=True))
    alpha = jnp.exp(m_prev - m_new)
    p = jnp.exp(s - m_new)
    l_ref[...] = alpha * l_ref[...] + jnp.sum(p, axis=-1, keepdims=True)
    acc_ref[...] = alpha * acc_ref[...] + jnp.dot(p.astype(BF), vp.astype(BF), preferred_element_type=F32)
    m_ref[...] = m_new

    @pl.when(p_idx == pl.num_programs(1) - 1)
    def _():
        kn = kn_ref[0]
        vn = vn_ref[0]
        kn2 = jnp.concatenate([kn, kn], axis=0)
        vn2 = jnp.concatenate([vn, vn], axis=0)
        s_n = jnp.sum(q16 * kn2, axis=-1, keepdims=True)
        m_prev = m_ref[...]
        m_fin = jnp.maximum(m_prev, s_n)
        alpha = jnp.exp(m_prev - m_fin)
        p_n = jnp.exp(s_n - m_fin)
        l_fin = alpha * l_ref[...] + p_n
        acc = alpha * acc_ref[...] + p_n * vn2
        lam = _diff_lambda(lq1_ref, lk1_ref, lq2_ref, lk2_ref, lam_init)
        o1 = acc[:heads] / l_fin[:heads]
        o2 = acc[heads:] / l_fin[heads:]
        o = o1 - lam * o2
        o = o * lax.rsqrt(jnp.mean(o * o, axis=-1, keepdims=True) + LN_EPS)
        o_ref[0] = (o * sg_ref[...] * (1.0 - lam_init)).astype(o_ref.dtype)


def _diff_decode(page_table, q, kn, vn, lq1, lk1, lq2, lk2, subln_g, cache_k, cache_v, *, heads, lam_init, name):
    bs, n_pages = page_table.shape
    hd = 2 * DIFF_D
    n_phys = cache_k.shape[0]
    ck = cache_k.reshape(n_phys, PAGE_SIZE * heads, hd)
    cv = cache_v.reshape(n_phys, PAGE_SIZE * heads, hd)
    q3 = q.reshape(bs, heads, hd)
    kn3 = kn.reshape(bs, heads, hd)
    vn3 = vn.reshape(bs, heads, hd)
    sg = subln_g.reshape(heads, hd)
    pt = page_table.reshape(-1)
    kern = functools.partial(_diff_decode_kernel, heads=heads, lam_init=lam_init)
    row_spec = pl.BlockSpec((1, heads, hd), lambda b, p, pt: (b, 0, 0))
    vec_spec = pl.BlockSpec((1, DIFF_D), lambda b, p, pt: (0, 0))
    page_spec = pl.BlockSpec((1, PAGE_SIZE * heads, hd), lambda b, p, pt: (pt[b * n_pages + p], 0, 0))
    grid_spec = pltpu.PrefetchScalarGridSpec(
        num_scalar_prefetch=1,
        grid=(bs, n_pages),
        in_specs=[row_spec, row_spec, row_spec, vec_spec, vec_spec, vec_spec, vec_spec,
                  pl.BlockSpec((heads, hd), lambda b, p, pt: (0, 0)),
                  page_spec, page_spec],
        out_specs=pl.BlockSpec((1, heads, hd), lambda b, p, pt: (b, 0, 0)),
        scratch_shapes=[pltpu.VMEM((2 * heads, 1), F32),
                        pltpu.VMEM((2 * heads, 1), F32),
                        pltpu.VMEM((2 * heads, hd), F32)],
    )
    out = pl.pallas_call(
        kern,
        grid_spec=grid_spec,
        out_shape=jax.ShapeDtypeStruct((bs, heads, hd), BF),
        compiler_params=_cparams(("parallel", "arbitrary")),
        name=name,
    )(pt, q3, kn3, vn3, lq1, lk1, lq2, lk2, sg, ck, cv)
    return out.reshape(bs, heads * hd)


def _ret_tables(pos):
    inv = 1.0 / (RET_ROPE_BASE ** jnp.linspace(0.0, 1.0, RET_DK // 2, dtype=F32))
    ang = pos.astype(F32)[:, None] * inv[None, :]
    return jnp.cos(ang), jnp.sin(ang)


def _diff_tables(pos):
    half = DIFF_ROT // 2
    inv = 1.0 / (ROPE_THETA ** (jnp.arange(half, dtype=F32) * 2.0 / DIFF_ROT))
    ang = pos.astype(F32)[:, None] * inv[None, :]
    cos, sin = jnp.cos(ang), jnp.sin(ang)
    n = pos.shape[0]
    ones = jnp.ones((n, DIFF_D - DIFF_ROT), F32)
    zeros_rest = jnp.zeros((n, DIFF_D - DIFF_ROT), F32)
    zeros_h = jnp.zeros((n, half), F32)
    c = jnp.concatenate([cos, cos, ones], axis=1)
    a = jnp.concatenate([-sin, zeros_h, zeros_rest], axis=1)
    b = jnp.concatenate([zeros_h, sin, zeros_rest], axis=1)
    rep = LANES // DIFF_D
    return tuple(jnp.tile(tab, (1, rep)) for tab in (c, a, b))


def _ret_split_weights(w_in, heads):
    hk, hv = heads * RET_DK, heads * RET_DV
    perm = np.concatenate([np.arange(0, RET_DK, 2), np.arange(1, RET_DK, 2)])
    cols = np.concatenate([h * RET_DK + perm for h in range(heads)])
    w_q = w_in[:, :hk][:, cols]
    w_k = w_in[:, hk:2 * hk][:, cols]
    w_qk = jnp.concatenate([w_q, w_k], axis=1).astype(BF)
    w_v = w_in[:, 2 * hk:2 * hk + hv].astype(BF)
    w_g = w_in[:, 2 * hk + hv:].astype(BF)
    return w_qk, w_v, w_g


def kernel(x_prompt, x_sample, state_ret, cache_k, cache_v, page_table, ret_w_in, ret_gn_g, ret_w_o, diff_w_in, diff_lq1, diff_lk1, diff_lq2, diff_lk2, diff_subln_g, diff_w_o, ffn_w_in, ffn_w_down, ln1_g, ln1_b, ln2_g, ln2_b):
    bp, tp, d_model = x_prompt.shape
    bs, ts, _ = x_sample.shape
    assert ts == 1
    depth = ffn_w_in.shape[0]
    ret_heads = ret_w_o.shape[1] // RET_DV
    diff_heads = diff_w_o.shape[1] // (2 * DIFF_D)
    hd = diff_heads * 2 * DIFF_D
    past_len = page_table.shape[1] * PAGE_SIZE
    alpha = (2.0 * depth) ** 0.25
    mp = bp * tp

    pos_p = jnp.arange(tp)
    pos_s = jnp.broadcast_to(past_len + jnp.arange(ts), (bs,))
    ret_tab_p = _ret_tables(pos_p)
    ret_tab_s = _ret_tables(pos_s)
    diff_tab_p = _diff_tables(pos_p)
    diff_tab_s = _diff_tables(pos_s)

    ret_chunk = 256 if tp % 256 == 0 else tp
    lg = jnp.log(1.0 - 2.0 ** (-5.0 - jnp.arange(ret_heads, dtype=F32)))
    lg_tab = jnp.broadcast_to(lg[:, None, None], (ret_heads, 1, max(ret_chunk, LANES)))

    tm_p = _row_tile(mp, 512)
    tm_s = bs

    xp = x_prompt.reshape(mp, d_model)
    xs = x_sample.reshape(bs * ts, d_model)
    xp_b, xs_b = xp, xs

    ret_p, ret_s, kc_p, vc_p, kc_s, vc_s = [], [], [], [], [], []
    for i in range(depth):
        j = i // N_MIXERS
        g1, b1 = ln1_g[i][None, :], ln1_b[i][None, :]
        g2, b2 = ln2_g[i][None, :], ln2_b[i][None, :]
        if i % N_MIXERS == 0:
            w_qk, w_v, w_g = _ret_split_weights(ret_w_in[j], ret_heads)
            w_o = ret_w_o[j].astype(BF)
            gn = ret_gn_g[j][None, :]
            qk = _ret_qk_proj(xp_b, w_qk, *ret_tab_p, BF, tm=tm_p, name=f"ret{j}_qk_p")
            (v,) = _proj_plain(xp_b, w_v, [BF], tm=tm_p, tn=512, name=f"ret{j}_v_p")
            (g,) = _proj_plain(xp_b, w_g, [F32], tm=tm_p, tn=512, name=f"ret{j}_g_p")
            mix_p, st_p = _ret_chunk(lg_tab, qk, v, g, gn, batch=bp, heads=ret_heads, chunk=ret_chunk,
                                     name=f"ret{j}_chunk_p")
            ret_p.append(st_p.reshape(bp, ret_heads, RET_DK, RET_DV))
            qk_s = _ret_qk_proj(xs_b, w_qk, *ret_tab_s, F32, tm=tm_s, name=f"ret{j}_qk_s")
            (v_s,) = _proj_plain(xs_b, w_v, [F32], tm=tm_s, tn=512, name=f"ret{j}_v_s")
            (g_s,) = _proj_plain(xs_b, w_g, [F32], tm=tm_s, tn=512, name=f"ret{j}_g_s")
            s0 = state_ret[j].reshape(bs, ret_heads, RET_DK // 2, 2 * RET_DV)
            mix_s, st_s = _ret_step(lg_tab, qk_s, v_s, g_s, gn, s0, heads=ret_heads, name=f"ret{j}_step_s")
            ret_s.append(st_s.reshape(bs, ret_heads, RET_DK, RET_DV))
        else:
            lam_init = 0.8 - 0.6 * math.exp(-0.3 * (i + 1))
            w_in = diff_w_in[j].astype(BF)
            w_q, w_k, w_v = w_in[:, :hd], w_in[:, hd:2 * hd], w_in[:, 2 * hd:]
            w_o = diff_w_o[j].astype(BF)
            lam_args = (diff_lq1[j][None, :], diff_lk1[j][None, :], diff_lq2[j][None, :], diff_lk2[j][None, :])
            sg = diff_subln_g[j][None, :]
            (q,) = _diff_rope_proj(xp_b, w_q, diff_tab_p, [BF], scale=DIFF_D ** -0.5, tm=tm_p, tn=512,
                                   name=f"diff{j}_q_p")
            k_f, k_b = _diff_rope_proj(xp_b, w_k, diff_tab_p, [F32, BF], scale=1.0, tm=tm_p, tn=512,
                                       name=f"diff{j}_k_p")
            v_f, v_b = _proj_plain(xp_b, w_v, [F32, BF], tm=tm_p, tn=512, name=f"diff{j}_v_p")
            tq = 512 if tp % 512 == 0 else tp
            mix_p = _diff_flash(q, k_b, v_b, *lam_args, sg, batch=bp, heads=diff_heads, tq=tq, tk=tq,
                                lam_init=lam_init, name=f"diff{j}_flash_p")
            kc_p.append(k_f.reshape(bp, tp, diff_heads, 2 * DIFF_D))
            vc_p.append(v_f.reshape(bp, tp, diff_heads, 2 * DIFF_D))
            (q_s,) = _diff_rope_proj(xs_b, w_q, diff_tab_s, [F32], scale=DIFF_D ** -0.5, tm=tm_s, tn=512,
                                     name=f"diff{j}_q_s")
            (kn,) = _diff_rope_proj(xs_b, w_k, diff_tab_s, [F32], scale=1.0, tm=tm_s, tn=512,
                                    name=f"diff{j}_k_s")
            (vn,) = _proj_plain(xs_b, w_v, [F32], tm=tm_s, tn=512, name=f"diff{j}_v_s")
            mix_s = _diff_decode(page_table, q_s, kn, vn, *lam_args, sg, cache_k[j], cache_v[j],
                                 heads=diff_heads, lam_init=lam_init, name=f"diff{j}_decode_s")
            kc_s.append(kn.reshape(bs, ts, diff_heads, 2 * DIFF_D))
            vc_s.append(vn.reshape(bs, ts, diff_heads, 2 * DIFF_D))

        xp, xp_b = _out_norm(mix_p, w_o, xp, g1, b1, alpha=alpha, tm=tm_p, name=f"l{i}_mix_norm_p")
        xs, xs_b = _out_norm(mix_s, w_o, xs, g1, b1, alpha=alpha, tm=tm_s, name=f"l{i}_mix_norm_s")
        w_fi = ffn_w_in[i].astype(BF)
        w_fd = ffn_w_down[i].astype(BF)
        xp, xp_b = _ffn(xp, xp_b, w_fi, w_fd, g2, b2, alpha=alpha, tm=_row_tile(mp, 1024), th=256,
                        name=f"l{i}_ffn_p")
        xs, xs_b = _ffn(xs, xs_b, w_fi, w_fd, g2, b2, alpha=alpha, tm=tm_s, th=256, name=f"l{i}_ffn_s")

    return (xp.reshape(bp, tp, d_model), xs.reshape(bs, ts, d_model),
            jnp.stack(ret_p), jnp.stack(kc_p), jnp.stack(vc_p),
            jnp.stack(ret_s), jnp.stack(kc_s), jnp.stack(vc_s))
```

```python
import functools
import math

import numpy as np
import jax
import jax.numpy as jnp
from jax import lax
from jax.experimental import pallas as pl
from jax.experimental.pallas import tpu as pltpu

BF = jnp.bfloat16
F32 = jnp.float32

RET_DK = 256
RET_DV = 512
RET_ROPE_BASE = 10000.0
DIFF_D = 64
DIFF_ROT = DIFF_D // 4
ROPE_THETA = 500000.0
PAGE_SIZE = 128
LN_EPS = 1e-5
N_MIXERS = 2

LANES = 128
FLASH_TQ = 1024
FLASH_TK = 1024
FLASH_ROWS = 256
DECODE_PAGES = 8
VMEM_LIMIT = 48 * 1024 * 1024


def _cparams(sem):
    return pltpu.CompilerParams(dimension_semantics=sem, vmem_limit_bytes=VMEM_LIMIT)


def _sigmoid(x):
    return 1.0 / (1.0 + jnp.exp(-x))


def _layer_norm(y, g, b):
    mu = jnp.mean(y, axis=-1, keepdims=True)
    d = y - mu
    var = jnp.mean(d * d, axis=-1, keepdims=True)
    return d * lax.rsqrt(var + LN_EPS) * g + b


def _row_tile(m, pref):
    return pref if m % pref == 0 else m


def _proj_plain_kernel(x_ref, w_ref, *o_refs):
    acc = jnp.dot(x_ref[...].astype(BF), w_ref[...], preferred_element_type=F32)
    for o_ref in o_refs:
        o_ref[...] = acc.astype(o_ref.dtype)


def _proj_plain(x, w, out_dtypes, *, tm, tn, name):
    m, k = x.shape
    n = w.shape[1]
    return pl.pallas_call(
        _proj_plain_kernel,
        grid=(m // tm, n // tn),
        in_specs=[pl.BlockSpec((tm, k), lambda i, j: (i, 0)),
                  pl.BlockSpec((k, tn), lambda i, j: (0, j))],
        out_specs=[pl.BlockSpec((tm, tn), lambda i, j: (i, j)) for _ in out_dtypes],
        out_shape=[jax.ShapeDtypeStruct((m, n), dt) for dt in out_dtypes],
        compiler_params=_cparams(("parallel", "parallel")),
        name=name,
    )(x, w)


def _ret_qk_kernel(x_ref, w_ref, cos_ref, sin_ref, o_ref, *, n_q_blocks, k_scale):
    acc = jnp.dot(x_ref[...].astype(BF), w_ref[...], preferred_element_type=F32)
    half = RET_DK // 2
    c = cos_ref[...]
    s = sin_ref[...]
    scale = jnp.where(pl.program_id(1) >= n_q_blocks, k_scale, 1.0).astype(F32)
    for h in range(acc.shape[1] // RET_DK):
        x1 = acc[:, h * RET_DK:h * RET_DK + half]
        x2 = acc[:, h * RET_DK + half:(h + 1) * RET_DK]
        o_ref[:, h * RET_DK:h * RET_DK + half] = ((x1 * c - x2 * s) * scale).astype(o_ref.dtype)
        o_ref[:, h * RET_DK + half:(h + 1) * RET_DK] = ((x2 * c + x1 * s) * scale).astype(o_ref.dtype)


def _ret_qk_proj(x, w_qk, cos, sin, out_dtype, *, tm, tn, name):
    m, k = x.shape
    n = w_qk.shape[1]
    nb = n // tn
    assert tn % RET_DK == 0 and nb % 2 == 0
    ntab = cos.shape[0] // tm
    kern = functools.partial(_ret_qk_kernel, n_q_blocks=nb // 2, k_scale=RET_DK ** -0.5)
    return pl.pallas_call(
        kern,
        grid=(m // tm, nb),
        in_specs=[pl.BlockSpec((tm, k), lambda i, j: (i, 0)),
                  pl.BlockSpec((k, tn), lambda i, j: (0, j)),
                  pl.BlockSpec((tm, RET_DK // 2), lambda i, j: (i % ntab, 0)),
                  pl.BlockSpec((tm, RET_DK // 2), lambda i, j: (i % ntab, 0))],
        out_specs=pl.BlockSpec((tm, tn), lambda i, j: (i, j)),
        out_shape=jax.ShapeDtypeStruct((m, n), out_dtype),
        compiler_params=_cparams(("parallel", "parallel")),
        name=name,
    )(x, w_qk, cos, sin)


def _shift_rope_kernel(x_ref, w_ref, c_ref, a_ref, b_ref, *o_refs, scale, shift):
    acc = jnp.dot(x_ref[...].astype(BF), w_ref[...], preferred_element_type=F32)
    tab_tiles = c_ref.shape[1] // LANES
    for t in range(acc.shape[1] // LANES):
        tt = t % tab_tiles
        cols = slice(tt * LANES, (tt + 1) * LANES)
        xs = acc[:, t * LANES:(t + 1) * LANES]
        up = pltpu.roll(xs, LANES - shift, axis=1)
        dn = pltpu.roll(xs, shift, axis=1)
        r = xs * c_ref[:, cols] + up * a_ref[:, cols] + dn * b_ref[:, cols]
        if scale != 1.0:
            r = r * scale
        for o_ref in o_refs:
            o_ref[:, t * LANES:(t + 1) * LANES] = r.astype(o_ref.dtype)


def _shift_rope_proj(x, w, tabs, out_dtypes, *, scale, shift, tm, tn, name):
    m, k = x.shape
    n = w.shape[1]
    ntab = tabs[0].shape[0] // tm
    tw = min(tabs[0].shape[1], tn)
    ntab_cols = tabs[0].shape[1] // tw
    kern = functools.partial(_shift_rope_kernel, scale=scale, shift=shift)
    tab_spec = pl.BlockSpec((tm, tw), lambda i, j: (i % ntab, j % ntab_cols))
    return pl.pallas_call(
        kern,
        grid=(m // tm, n // tn),
        in_specs=[pl.BlockSpec((tm, k), lambda i, j: (i, 0)),
                  pl.BlockSpec((k, tn), lambda i, j: (0, j)),
                  tab_spec, tab_spec, tab_spec],
        out_specs=[pl.BlockSpec((tm, tn), lambda i, j: (i, j)) for _ in out_dtypes],
        out_shape=[jax.ShapeDtypeStruct((m, n), dt) for dt in out_dtypes],
        compiler_params=_cparams(("parallel", "parallel")),
        name=name,
    )(x, w, *tabs)


def _out_norm_kernel(a_ref, w_ref, x_ref, g_ref, b_ref, o_ref, ob_ref, *, alpha):
    sub = jnp.dot(a_ref[...].astype(BF), w_ref[...], preferred_element_type=F32)
    y = _layer_norm(alpha * x_ref[...] + sub, g_ref[...], b_ref[...])
    o_ref[...] = y
    ob_ref[...] = y.astype(BF)


def _out_norm(a, w, x, g, b, *, alpha, tm, name):
    m, ka = a.shape
    d = w.shape[1]
    kern = functools.partial(_out_norm_kernel, alpha=alpha)
    return pl.pallas_call(
        kern,
        grid=(m // tm,),
        in_specs=[pl.BlockSpec((tm, ka), lambda i: (i, 0)),
                  pl.BlockSpec((ka, d), lambda i: (0, 0)),
                  pl.BlockSpec((tm, d), lambda i: (i, 0)),
                  pl.BlockSpec((1, d), lambda i: (0, 0)),
                  pl.BlockSpec((1, d), lambda i: (0, 0))],
        out_specs=[pl.BlockSpec((tm, d), lambda i: (i, 0)),
                   pl.BlockSpec((tm, d), lambda i: (i, 0))],
        out_shape=[jax.ShapeDtypeStruct((m, d), F32), jax.ShapeDtypeStruct((m, d), BF)],
        compiler_params=_cparams(("parallel",)),
        name=name,
    )(a, w, x, g, b)


def _ffn_kernel(x_ref, xb_ref, wg_ref, wu_ref, wd_ref, g_ref, b_ref, o_ref, ob_ref, acc_ref, *, alpha):
    j = pl.program_id(1)

    @pl.when(j == 0)
    def _():
        acc_ref[...] = jnp.zeros_like(acc_ref)

    xb = xb_ref[...]
    gate = jnp.dot(xb, wg_ref[...], preferred_element_type=F32)
    up = jnp.dot(xb, wu_ref[...], preferred_element_type=F32)
    h = gate * _sigmoid(gate) * up
    acc_ref[...] += jnp.dot(h.astype(BF), wd_ref[...], preferred_element_type=F32)

    @pl.when(j == pl.num_programs(1) - 1)
    def _():
        y = _layer_norm(alpha * x_ref[...] + acc_ref[...], g_ref[...], b_ref[...])
        o_ref[...] = y
        ob_ref[...] = y.astype(BF)


def _ffn(x, xb, w_in, w_down, g, b, *, alpha, tm, th, name):
    m, d = x.shape
    hidden = w_down.shape[0]
    nh = hidden // th
    kern = functools.partial(_ffn_kernel, alpha=alpha)
    return pl.pallas_call(
        kern,
        grid=(m // tm, nh),
        in_specs=[pl.BlockSpec((tm, d), lambda i, j: (i, 0)),
                  pl.BlockSpec((tm, d), lambda i, j: (i, 0)),
                  pl.BlockSpec((d, th), lambda i, j: (0, j)),
                  pl.BlockSpec((d, th), lambda i, j: (0, nh + j)),
                  pl.BlockSpec((th, d), lambda i, j: (j, 0)),
                  pl.BlockSpec((1, d), lambda i, j: (0, 0)),
                  pl.BlockSpec((1, d), lambda i, j: (0, 0))],
        out_specs=[pl.BlockSpec((tm, d), lambda i, j: (i, 0)),
                   pl.BlockSpec((tm, d), lambda i, j: (i, 0))],
        out_shape=[jax.ShapeDtypeStruct((m, d), F32), jax.ShapeDtypeStruct((m, d), BF)],
        scratch_shapes=[pltpu.VMEM((tm, d), F32)],
        compiler_params=_cparams(("parallel", "arbitrary")),
        name=name,
    )(x, xb, w_in, w_in, w_down, g, b)


def _ret_chunk_kernel(lg_ref, q_ref, k_ref, v_ref, g_ref, gn_ref, o_ref, st_ref, s_ref, *, chunk):
    c = pl.program_id(2)
    half = RET_DK // 2

    @pl.when(c == 0)
    def _():
        s_ref[...] = jnp.zeros_like(s_ref)

    lg_row = lg_ref[0]
    lg = lg_row[:, :1]
    ii = lax.broadcasted_iota(jnp.int32, (chunk, chunk), 0)
    jj = lax.broadcasted_iota(jnp.int32, (chunk, chunk), 1)
    rel = (ii - jj).astype(F32)
    decay = jnp.where(rel >= 0, jnp.exp(lg_row * jnp.maximum(rel, 0.0)), 0.0)
    idx = lax.broadcasted_iota(jnp.int32, (chunk, 1), 0).astype(F32)
    q_decay = jnp.exp(lg * (idx + 1.0))
    k_decay = jnp.exp(lg * (chunk - 1.0 - idx))
    chunk_decay = jnp.exp(lg * float(chunk))

    q = q_ref[...]
    k = k_ref[...]
    v = v_ref[...]
    s_old = s_ref[...]
    scores = lax.dot_general(q, k, (((1,), (1,)), ((), ())), preferred_element_type=F32) * decay
    inner = jnp.dot(scores.astype(BF), v, preferred_element_type=F32)
    cross = jnp.dot(q, s_old.astype(BF), preferred_element_type=F32) * q_decay
    kd_t = (k.astype(F32) * k_decay).T.astype(BF)
    s_ref[...] = chunk_decay * s_old + jnp.dot(kd_t, v, preferred_element_type=F32)

    o = inner + cross
    mu = jnp.mean(o, axis=-1, keepdims=True)
    d = o - mu
    var = jnp.mean(d * d, axis=-1, keepdims=True)
    on = d * lax.rsqrt(var + LN_EPS) * gn_ref[...]
    gt = g_ref[...].astype(F32)
    o_ref[...] = (gt * _sigmoid(gt) * on).astype(o_ref.dtype)

    @pl.when(c == pl.num_programs(2) - 1)
    def _():
        st_ref[0, 0, :, :RET_DV] = s_ref[:half, :]
        st_ref[0, 0, :, RET_DV:] = s_ref[half:, :]


def _ret_chunk(lg_tab, qk, v, g, gn_g, *, batch, heads, chunk, name):
    m = qk.shape[0]
    t = m // batch
    nc = t // chunk
    kern = functools.partial(_ret_chunk_kernel, chunk=chunk)
    return pl.pallas_call(
        kern,
        grid=(batch, heads, nc),
        in_specs=[pl.BlockSpec((1, 1, chunk), lambda b, h, c: (h, 0, 0)),
                  pl.BlockSpec((chunk, RET_DK), lambda b, h, c: (b * nc + c, h)),
                  pl.BlockSpec((chunk, RET_DK), lambda b, h, c: (b * nc + c, heads + h)),
                  pl.BlockSpec((chunk, RET_DV), lambda b, h, c: (b * nc + c, h)),
                  pl.BlockSpec((chunk, RET_DV), lambda b, h, c: (b * nc + c, h)),
                  pl.BlockSpec((1, RET_DV), lambda b, h, c: (0, h))],
        out_specs=[pl.BlockSpec((chunk, RET_DV), lambda b, h, c: (b * nc + c, h)),
                   pl.BlockSpec((1, 1, RET_DK // 2, 2 * RET_DV), lambda b, h, c: (b, h, 0, 0))],
        out_shape=[jax.ShapeDtypeStruct((m, heads * RET_DV), BF),
                   jax.ShapeDtypeStruct((batch, heads, RET_DK // 2, 2 * RET_DV), F32)],
        scratch_shapes=[pltpu.VMEM((RET_DK, RET_DV), F32)],
        compiler_params=_cparams(("parallel", "parallel", "arbitrary")),
        name=name,
    )(lg_tab, qk, qk, v, g, gn_g)


def _ret_step_kernel(lg_ref, q_ref, k_ref, v_ref, g_ref, gn_ref, s0_ref, o_ref, st_ref, *, heads):
    qrow = q_ref[0]
    krow = k_ref[0]
    vrow = v_ref[0]
    grow = g_ref[0]
    for h in range(heads):
        gamma = jnp.exp(lg_ref[h][:, :1])
        qh = qrow[:, h * RET_DK:(h + 1) * RET_DK]
        kh = krow[:, h * RET_DK:(h + 1) * RET_DK]
        vh = vrow[:, h * RET_DV:(h + 1) * RET_DV]
        s0 = s0_ref[0, 0, h]
        k_col = jnp.broadcast_to(kh, (8, RET_DK)).T[:, :1]
        st_ref[0, h] = gamma * s0 + k_col * vh
        q8 = jnp.broadcast_to(qh, (8, RET_DK)).astype(BF)
        cross = jnp.dot(q8, s0.astype(BF), preferred_element_type=F32)[:1]
        qk = jnp.sum(qh * kh, axis=-1, keepdims=True)
        o = qk * vh + gamma * cross
        mu = jnp.mean(o, axis=-1, keepdims=True)
        d = o - mu
        var = jnp.mean(d * d, axis=-1, keepdims=True)
        on = d * lax.rsqrt(var + LN_EPS) * gn_ref[:, h * RET_DV:(h + 1) * RET_DV]
        gt = grow[:, h * RET_DV:(h + 1) * RET_DV]
        o_ref[0, :, h * RET_DV:(h + 1) * RET_DV] = (gt * _sigmoid(gt) * on).astype(o_ref.dtype)


def _ret_step(lg_tab, qk, v, g, gn_g, state_all, layer, *, heads, name):
    bs = qk.shape[0]
    hk = heads * RET_DK
    hv = heads * RET_DV
    q3 = qk.reshape(bs, 1, 2 * hk)
    v3 = v.reshape(bs, 1, hv)
    g3 = g.reshape(bs, 1, hv)
    kern = functools.partial(_ret_step_kernel, heads=heads)
    out, st = pl.pallas_call(
        kern,
        grid=(bs,),
        in_specs=[pl.BlockSpec(lg_tab.shape, lambda b: (0, 0, 0)),
                  pl.BlockSpec((1, 1, hk), lambda b: (b, 0, 0)),
                  pl.BlockSpec((1, 1, hk), lambda b: (b, 0, 1)),
                  pl.BlockSpec((1, 1, hv), lambda b: (b, 0, 0)),
                  pl.BlockSpec((1, 1, hv), lambda b: (b, 0, 0)),
                  pl.BlockSpec((1, hv), lambda b: (0, 0)),
                  pl.BlockSpec((1, 1, heads, RET_DK, RET_DV), lambda b: (layer, b, 0, 0, 0))],
        out_specs=[pl.BlockSpec((1, 1, hv), lambda b: (b, 0, 0)),
                   pl.BlockSpec((1, heads, RET_DK, RET_DV), lambda b: (b, 0, 0, 0))],
        out_shape=[jax.ShapeDtypeStruct((bs, 1, hv), BF),
                   jax.ShapeDtypeStruct(state_all.shape[1:], F32)],
        compiler_params=_cparams(("parallel",)),
        name=name,
    )(lg_tab, q3, q3, v3, g3, gn_g, state_all)
    return out.reshape(bs, hv), st


def _diff_lambda(lq1_ref, lk1_ref, lq2_ref, lk2_ref, lam_init):
    a = jnp.sum(lq1_ref[...] * lk1_ref[...], axis=-1, keepdims=True)
    b = jnp.sum(lq2_ref[...] * lk2_ref[...], axis=-1, keepdims=True)
    return jnp.exp(a) - jnp.exp(b) + lam_init


def _diff_flash_kernel(qi_ref, kj_ref, q_ref, k_ref, v_ref, lq1_ref, lk1_ref, lq2_ref, lk2_ref, sg_ref,
                       o_ref, q1_ref, q2_ref, m1_ref, m2_ref, l1_ref, l2_ref, a1_ref, a2_ref,
                       *, tq, tk, rb, lam_init):
    step = pl.program_id(2)
    qi = qi_ref[step]
    kj = kj_ref[step]
    last_kj = ((qi + 1) * tq - 1) // tk
    comps = ((q1_ref, m1_ref, l1_ref, a1_ref), (q2_ref, m2_ref, l2_ref, a2_ref))

    @pl.when(kj == 0)
    def _():
        q = q_ref[...]
        lane = lax.broadcasted_iota(jnp.int32, q.shape, 1)
        zero = jnp.zeros_like(q)
        q1_ref[...] = jnp.where(lane < DIFF_D, q, zero)
        q2_ref[...] = jnp.where(lane >= DIFF_D, q, zero)
        for _, m_ref, l_ref, a_ref in comps:
            m_ref[...] = jnp.full_like(m_ref, -jnp.inf)
            l_ref[...] = jnp.zeros_like(l_ref)
            a_ref[...] = jnp.zeros_like(a_ref)

    def update(masked):
        k = k_ref[...]
        v = v_ref[...]
        v_ext = jnp.concatenate([v, jnp.ones_like(v)], axis=1)
        for qc_ref, m_ref, l_ref, a_ref in comps:
            for r in range(tq // rb):
                rows = pl.ds(r * rb, rb)
                w = min((r + 1) * rb, tk) if (masked and tq == tk) else tk
                s = lax.dot_general(qc_ref[rows, :], k[:w], (((1,), (1,)), ((), ())),
                                    preferred_element_type=F32)
                if masked:
                    row = lax.broadcasted_iota(jnp.int32, (rb, w), 0) + (qi * tq + r * rb)
                    col = lax.broadcasted_iota(jnp.int32, (rb, w), 1) + kj * tk
                    s = jnp.where(col <= row, s, -jnp.inf)
                m_prev = m_ref[rows, :]
                m_new = jnp.maximum(m_prev, jnp.max(s, axis=-1, keepdims=True))
                alpha = jnp.exp2(m_prev - m_new)
                p = jnp.exp2(s - jnp.tile(m_new, (1, w // LANES)))
                pv = jnp.dot(p.astype(BF), v_ext[:w], preferred_element_type=F32)
                a_ref[rows, :] = alpha * a_ref[rows, :] + pv[:, :LANES]
                l_ref[rows, :] = alpha * l_ref[rows, :] + pv[:, LANES:]
                m_ref[rows, :] = m_new

    straddles = (kj + 1) * tk - 1 > qi * tq

    @pl.when(straddles)
    def _():
        update(True)

    @pl.when(jnp.logical_not(straddles))
    def _():
        update(False)

    @pl.when(kj == last_kj)
    def _():
        lam = _diff_lambda(lq1_ref, lk1_ref, lq2_ref, lk2_ref, lam_init)
        o = a1_ref[...] / l1_ref[...] - lam * (a2_ref[...] / l2_ref[...])
        o = o * lax.rsqrt(jnp.mean(o * o, axis=-1, keepdims=True) + LN_EPS)
        o_ref[...] = (o * sg_ref[...] * (1.0 - lam_init)).astype(o_ref.dtype)


def _diff_flash(q, k, v, lq1, lk1, lq2, lk2, subln_g, *, batch, heads, tq, tk, rb, lam_init, name):
    m = q.shape[0]
    t = m // batch
    nq, nk = t // tq, t // tk
    qi_list, kj_list = [], []
    for i in range(nq):
        for j in range(((i + 1) * tq - 1) // tk + 1):
            qi_list.append(i)
            kj_list.append(j)
    qi_tab = jnp.asarray(np.array(qi_list, np.int32))
    kj_tab = jnp.asarray(np.array(kj_list, np.int32))
    hd = 2 * DIFF_D
    assert hd == LANES
    kern = functools.partial(_diff_flash_kernel, tq=tq, tk=tk, rb=rb, lam_init=lam_init)
    vec_spec = pl.BlockSpec((1, DIFF_D), lambda b, h, s, qi, kj: (0, 0))
    grid_spec = pltpu.PrefetchScalarGridSpec(
        num_scalar_prefetch=2,
        grid=(batch, heads, len(qi_list)),
        in_specs=[pl.BlockSpec((tq, hd), lambda b, h, s, qi, kj: (b * nq + qi[s], h)),
                  pl.BlockSpec((tk, hd), lambda b, h, s, qi, kj: (b * nk + kj[s], h)),
                  pl.BlockSpec((tk, hd), lambda b, h, s, qi, kj: (b * nk + kj[s], h)),
                  vec_spec, vec_spec, vec_spec, vec_spec,
                  pl.BlockSpec((1, hd), lambda b, h, s, qi, kj: (0, h))],
        out_specs=pl.BlockSpec((tq, hd), lambda b, h, s, qi, kj: (b * nq + qi[s], h)),
        scratch_shapes=[pltpu.VMEM((tq, hd), BF), pltpu.VMEM((tq, hd), BF)]
                       + [pltpu.VMEM((tq, hd), F32) for _ in range(6)],
    )
    return pl.pallas_call(
        kern,
        grid_spec=grid_spec,
        out_shape=jax.ShapeDtypeStruct((m, heads * hd), BF),
        compiler_params=_cparams(("parallel", "parallel", "arbitrary")),
        name=name,
    )(qi_tab, kj_tab, q, k, v, lq1, lk1, lq2, lk2, subln_g)


def _diff_decode_kernel(pt_ref, q_ref, kn_ref, vn_ref, lq1_ref, lk1_ref, lq2_ref, lk2_ref, sg_ref, *rest,
                        heads, n_fetch, lam_init):
    del pt_ref
    k_refs = rest[:n_fetch]
    v_refs = rest[n_fetch:2 * n_fetch]
    o_ref, m_ref, l_ref, acc_ref = rest[2 * n_fetch:]
    p_idx = pl.program_id(1)
    rows = 2 * heads

    @pl.when(p_idx == 0)
    def _():
        m_ref[...] = jnp.full_like(m_ref, -jnp.inf)
        l_ref[...] = jnp.zeros_like(l_ref)
        acc_ref[...] = jnp.zeros_like(acc_ref)

    q8 = q_ref[0]
    lane = lax.broadcasted_iota(jnp.int32, q8.shape, 1)
    q16 = jnp.concatenate([jnp.where(lane < DIFF_D, q8, 0.0),
                           jnp.where(lane >= DIFF_D, q8, 0.0)], axis=0)
    q16b = q16.astype(BF)

    ncol = k_refs[0].shape[1]
    r_head = lax.broadcasted_iota(jnp.int32, (rows, ncol), 0) % heads
    c_head = lax.broadcasted_iota(jnp.int32, (rows, ncol), 1) % heads
    same_head = r_head == c_head
    s_pages = []
    for k_ref in k_refs:
        s = lax.dot_general(q16b, k_ref[0].astype(BF), (((1,), (1,)), ((), ())),
                            preferred_element_type=F32)
        s_pages.append(jnp.where(same_head, s, -jnp.inf))
    m_prev = m_ref[...]
    m_new = m_prev
    for s in s_pages:
        m_new = jnp.maximum(m_new, jnp.max(s, axis=-1, keepdims=True))
    alpha = jnp.exp(m_prev - m_new)
    l_new = alpha * l_ref[...]
    acc = alpha * acc_ref[...]
    for s, v_ref in zip(s_pages, v_refs):
        p = jnp.exp(s - m_new)
        l_new = l_new + jnp.sum(p, axis=-1, keepdims=True)
        acc = acc + jnp.dot(p.astype(BF), v_ref[0].astype(BF), preferred_element_type=F32)
    l_ref[...] = l_new
    acc_ref[...] = acc
    m_ref[...] = m_new

    @pl.when(p_idx == pl.num_programs(1) - 1)
    def _():
        kn = kn_ref[0]
        vn = vn_ref[0]
        kn2 = jnp.concatenate([kn, kn], axis=0)
        vn2 = jnp.concatenate([vn, vn], axis=0)
        s_n = jnp.sum(q16 * kn2, axis=-1, keepdims=True)
        m_fin = jnp.maximum(m_new, s_n)
        beta = jnp.exp(m_new - m_fin)
        p_n = jnp.exp(s_n - m_fin)
        l_fin = beta * l_new + p_n
        acc_fin = beta * acc + p_n * vn2
        lam = _diff_lambda(lq1_ref, lk1_ref, lq2_ref, lk2_ref, lam_init)
        o = acc_fin[:heads] / l_fin[:heads] - lam * (acc_fin[heads:] / l_fin[heads:])
        o = o * lax.rsqrt(jnp.mean(o * o, axis=-1, keepdims=True) + LN_EPS)
        o_ref[0] = (o * sg_ref[...] * (1.0 - lam_init)).astype(o_ref.dtype)


def _diff_decode(page_table, q, kn, vn, lq1, lk1, lq2, lk2, subln_g, cache_k, cache_v, layer, *,
                 heads, lam_init, name):
    bs, n_pages = page_table.shape
    hd = 2 * DIFF_D
    n_layers, n_phys = cache_k.shape[:2]
    ck = cache_k.reshape(n_layers * n_phys, PAGE_SIZE * heads, hd)
    cv = cache_v.reshape(n_layers * n_phys, PAGE_SIZE * heads, hd)
    q3 = q.reshape(bs, heads, hd)
    kn3 = kn.reshape(bs, heads, hd)
    vn3 = vn.reshape(bs, heads, hd)
    sg = subln_g.reshape(heads, hd)
    pt = page_table.reshape(-1)
    n_fetch = DECODE_PAGES if n_pages % DECODE_PAGES == 0 else 1
    base = layer * n_phys
    kern = functools.partial(_diff_decode_kernel, heads=heads, n_fetch=n_fetch, lam_init=lam_init)
    row_spec = pl.BlockSpec((1, heads, hd), lambda b, p, pt: (b, 0, 0))
    vec_spec = pl.BlockSpec((1, DIFF_D), lambda b, p, pt: (0, 0))

    def page_spec(r):
        return pl.BlockSpec((1, PAGE_SIZE * heads, hd),
                            lambda b, p, pt: (base + pt[b * n_pages + p * n_fetch + r], 0, 0))

    page_specs = [page_spec(r) for r in range(n_fetch)]
    grid_spec = pltpu.PrefetchScalarGridSpec(
        num_scalar_prefetch=1,
        grid=(bs, n_pages // n_fetch),
        in_specs=[row_spec, row_spec, row_spec, vec_spec, vec_spec, vec_spec, vec_spec,
                  pl.BlockSpec((heads, hd), lambda b, p, pt: (0, 0))] + page_specs + page_specs,
        out_specs=pl.BlockSpec((1, heads, hd), lambda b, p, pt: (b, 0, 0)),
        scratch_shapes=[pltpu.VMEM((2 * heads, 1), F32),
                        pltpu.VMEM((2 * heads, 1), F32),
                        pltpu.VMEM((2 * heads, hd), F32)],
    )
    out = pl.pallas_call(
        kern,
        grid_spec=grid_spec,
        out_shape=jax.ShapeDtypeStruct((bs, heads, hd), BF),
        compiler_params=_cparams(("parallel", "arbitrary")),
        name=name,
    )(pt, q3, kn3, vn3, lq1, lk1, lq2, lk2, sg, *([ck] * n_fetch), *([cv] * n_fetch))
    return out.reshape(bs, heads * hd)


def _ret_tables(pos):
    inv = 1.0 / (RET_ROPE_BASE ** jnp.linspace(0.0, 1.0, RET_DK // 2, dtype=F32))
    ang = pos.astype(F32)[:, None] * inv[None, :]
    return jnp.cos(ang), jnp.sin(ang)


def _ret_tables_interleaved(pos, heads):
    cos, sin = _ret_tables(pos)
    zero = jnp.zeros_like(sin)
    c = jnp.stack([cos, cos], axis=-1).reshape(pos.shape[0], RET_DK)
    a = jnp.stack([-sin, zero], axis=-1).reshape(pos.shape[0], RET_DK)
    b = jnp.stack([zero, sin], axis=-1).reshape(pos.shape[0], RET_DK)
    k_scale = RET_DK ** -0.5
    return tuple(jnp.concatenate([jnp.tile(tab, (1, heads)), jnp.tile(tab * k_scale, (1, heads))], axis=1)
                 for tab in (c, a, b))


def _diff_tables(pos):
    half = DIFF_ROT // 2
    inv = 1.0 / (ROPE_THETA ** (jnp.arange(half, dtype=F32) * 2.0 / DIFF_ROT))
    ang = pos.astype(F32)[:, None] * inv[None, :]
    cos, sin = jnp.cos(ang), jnp.sin(ang)
    n = pos.shape[0]
    ones = jnp.ones((n, DIFF_D - DIFF_ROT), F32)
    zeros_rest = jnp.zeros((n, DIFF_D - DIFF_ROT), F32)
    zeros_h = jnp.zeros((n, half), F32)
    c = jnp.concatenate([cos, cos, ones], axis=1)
    a = jnp.concatenate([-sin, zeros_h, zeros_rest], axis=1)
    b = jnp.concatenate([zeros_h, sin, zeros_rest], axis=1)
    rep = LANES // DIFF_D
    return tuple(jnp.tile(tab, (1, rep)) for tab in (c, a, b))


def _ret_split_weights(w_in, heads):
    hk, hv = heads * RET_DK, heads * RET_DV
    perm = np.concatenate([np.arange(0, RET_DK, 2), np.arange(1, RET_DK, 2)])
    cols = np.concatenate([h * RET_DK + perm for h in range(heads)])
    w_q = w_in[:, :hk][:, cols]
    w_k = w_in[:, hk:2 * hk][:, cols]
    w_qk = jnp.concatenate([w_q, w_k], axis=1).astype(BF)
    w_v = w_in[:, 2 * hk:2 * hk + hv].astype(BF)
    w_g = w_in[:, 2 * hk + hv:].astype(BF)
    return w_qk, w_v, w_g


def kernel(x_prompt, x_sample, state_ret, cache_k, cache_v, page_table, ret_w_in, ret_gn_g, ret_w_o, diff_w_in, diff_lq1, diff_lk1, diff_lq2, diff_lk2, diff_subln_g, diff_w_o, ffn_w_in, ffn_w_down, ln1_g, ln1_b, ln2_g, ln2_b):
    bp, tp, d_model = x_prompt.shape
    bs, ts, _ = x_sample.shape
    assert ts == 1
    depth = ffn_w_in.shape[0]
    ret_heads = ret_w_o.shape[1] // RET_DV
    diff_heads = diff_w_o.shape[1] // (2 * DIFF_D)
    hd = diff_heads * 2 * DIFF_D
    past_len = page_table.shape[1] * PAGE_SIZE
    alpha = (2.0 * depth) ** 0.25
    mp = bp * tp

    pos_p = jnp.arange(tp)
    pos_s = jnp.broadcast_to(past_len + jnp.arange(ts), (bs,))
    ret_tab_p = _ret_tables(pos_p)
    ret_tab_s = _ret_tables_interleaved(pos_s, ret_heads)
    diff_tab_p = _diff_tables(pos_p)
    diff_tab_s = _diff_tables(pos_s)

    ret_chunk = 256 if tp % 256 == 0 else tp
    lg = jnp.log(1.0 - 2.0 ** (-5.0 - jnp.arange(ret_heads, dtype=F32)))
    lg_tab = jnp.broadcast_to(lg[:, None, None], (ret_heads, 1, max(ret_chunk, LANES)))

    tm_p = _row_tile(mp, 1024)
    tm_o = _row_tile(mp, 512)
    tm_s = bs

    xp = x_prompt.reshape(mp, d_model)
    xs = x_sample.reshape(bs * ts, d_model)
    xp_b, xs_b = xp, xs

    ret_p, ret_s, kc_p, vc_p, kc_s, vc_s = [], [], [], [], [], []
    for i in range(depth):
        j = i // N_MIXERS
        g1, b1 = ln1_g[i][None, :], ln1_b[i][None, :]
        g2, b2 = ln2_g[i][None, :], ln2_b[i][None, :]
        if i % N_MIXERS == 0:
            w_qk, w_v, w_g = _ret_split_weights(ret_w_in[j], ret_heads)
            w_o = ret_w_o[j].astype(BF)
            gn = ret_gn_g[j][None, :]
            qk = _ret_qk_proj(xp_b, w_qk, *ret_tab_p, BF, tm=tm_p, tn=512, name=f"ret{j}_qk_p")
            (v,) = _proj_plain(xp_b, w_v, [BF], tm=tm_p, tn=512, name=f"ret{j}_v_p")
            (g,) = _proj_plain(xp_b, w_g, [F32], tm=tm_p, tn=512, name=f"ret{j}_g_p")
            mix_p, st_p = _ret_chunk(lg_tab, qk, v, g, gn, batch=bp, heads=ret_heads, chunk=ret_chunk,
                                     name=f"ret{j}_chunk_p")
            ret_p.append(st_p.reshape(bp, ret_heads, RET_DK, RET_DV))
            w_qk_nat = ret_w_in[j][:, :2 * ret_heads * RET_DK].astype(BF)
            (qk_s,) = _shift_rope_proj(xs_b, w_qk_nat, ret_tab_s, [F32], scale=1.0, shift=1, tm=tm_s, tn=512,
                                       name=f"ret{j}_qk_s")
            (v_s,) = _proj_plain(xs_b, w_v, [F32], tm=tm_s, tn=512, name=f"ret{j}_v_s")
            (g_s,) = _proj_plain(xs_b, w_g, [F32], tm=tm_s, tn=512, name=f"ret{j}_g_s")
            mix_s, st_s = _ret_step(lg_tab, qk_s, v_s, g_s, gn, state_ret, j, heads=ret_heads,
                                    name=f"ret{j}_step_s")
            ret_s.append(st_s)
        else:
            lam_init = 0.8 - 0.6 * math.exp(-0.3 * (i + 1))
            w_in = diff_w_in[j].astype(BF)
            w_q, w_k, w_v = w_in[:, :hd], w_in[:, hd:2 * hd], w_in[:, 2 * hd:]
            w_o = diff_w_o[j].astype(BF)
            lam_args = (diff_lq1[j][None, :], diff_lk1[j][None, :], diff_lq2[j][None, :], diff_lk2[j][None, :])
            sg = diff_subln_g[j][None, :]
            rot = DIFF_ROT // 2
            (q,) = _shift_rope_proj(xp_b, w_q, diff_tab_p, [BF], scale=DIFF_D ** -0.5 * math.log2(math.e),
                                    shift=rot, tm=tm_p, tn=512, name=f"diff{j}_q_p")
            k_f, k_b = _shift_rope_proj(xp_b, w_k, diff_tab_p, [F32, BF], scale=1.0, shift=rot, tm=tm_p, tn=512,
                                        name=f"diff{j}_k_p")
            v_f, v_b = _proj_plain(xp_b, w_v, [F32, BF], tm=tm_p, tn=512, name=f"diff{j}_v_p")
            tq = FLASH_TQ if tp % FLASH_TQ == 0 else tp
            tk = FLASH_TK if tp % FLASH_TK == 0 else tp
            mix_p = _diff_flash(q, k_b, v_b, *lam_args, sg, batch=bp, heads=diff_heads, tq=tq, tk=tk,
                                rb=min(FLASH_ROWS, tq), lam_init=lam_init, name=f"diff{j}_flash_p")
            kc_p.append(k_f.reshape(bp, tp, diff_heads, 2 * DIFF_D))
            vc_p.append(v_f.reshape(bp, tp, diff_heads, 2 * DIFF_D))
            (q_s,) = _shift_rope_proj(xs_b, w_q, diff_tab_s, [F32], scale=DIFF_D ** -0.5, shift=rot, tm=tm_s, tn=512,
                                      name=f"diff{j}_q_s")
            (kn,) = _shift_rope_proj(xs_b, w_k, diff_tab_s, [F32], scale=1.0, shift=rot, tm=tm_s, tn=512,
                                     name=f"diff{j}_k_s")
            (vn,) = _proj_plain(xs_b, w_v, [F32], tm=tm_s, tn=512, name=f"diff{j}_v_s")
            mix_s = _diff_decode(page_table, q_s, kn, vn, *lam_args, sg, cache_k, cache_v, j,
                                 heads=diff_heads, lam_init=lam_init, name=f"diff{j}_decode_s")
            kc_s.append(kn.reshape(bs, ts, diff_heads, 2 * DIFF_D))
            vc_s.append(vn.reshape(bs, ts, diff_heads, 2 * DIFF_D))

        xp, xp_b = _out_norm(mix_p, w_o, xp, g1, b1, alpha=alpha, tm=tm_o, name=f"l{i}_mix_norm_p")
        xs, xs_b = _out_norm(mix_s, w_o, xs, g1, b1, alpha=alpha, tm=tm_s, name=f"l{i}_mix_norm_s")
        w_fi = ffn_w_in[i].astype(BF)
        w_fd = ffn_w_down[i].astype(BF)
        xp, xp_b = _ffn(xp, xp_b, w_fi, w_fd, g2, b2, alpha=alpha, tm=_row_tile(mp, 1024), th=256,
                        name=f"l{i}_ffn_p")
        xs, xs_b = _ffn(xs, xs_b, w_fi, w_fd, g2, b2, alpha=alpha, tm=tm_s, th=256, name=f"l{i}_ffn_s")

    return (xp.reshape(bp, tp, d_model), xs.reshape(bs, ts, d_model),
            jnp.stack(ret_p), jnp.stack(kc_p), jnp.stack(vc_p),
            jnp.stack(ret_s), jnp.stack(kc_s), jnp.stack(vc_s))
```

```python
import functools
import math

import numpy as np
import jax
import jax.numpy as jnp
from jax import lax
from jax.experimental import pallas as pl
from jax.experimental.pallas import tpu as pltpu

BF = jnp.bfloat16
F32 = jnp.float32

RET_DK = 256
RET_DV = 512
RET_ROPE_BASE = 10000.0
DIFF_D = 64
DIFF_ROT = DIFF_D // 4
ROPE_THETA = 500000.0
PAGE_SIZE = 128
LN_EPS = 1e-5
N_MIXERS = 2

LANES = 128
FLASH_TQ = 1024
FLASH_TK = 1024
FLASH_ROWS = 256
RET_CHUNK = 512
FFN_ROWS = 512
DECODE_PAGES = 8
VMEM_LIMIT = 48 * 1024 * 1024


def _cparams(sem):
    return pltpu.CompilerParams(dimension_semantics=sem, vmem_limit_bytes=VMEM_LIMIT)


def _sigmoid(x):
    return 1.0 / (1.0 + jnp.exp(-x))


def _layer_norm(y, g, b):
    mu = jnp.mean(y, axis=-1, keepdims=True)
    d = y - mu
    var = jnp.mean(d * d, axis=-1, keepdims=True)
    return d * lax.rsqrt(var + LN_EPS) * g + b


def _row_tile(m, pref):
    return pref if m % pref == 0 else m


def _proj_plain_kernel(x_ref, w_ref, *o_refs):
    acc = jnp.dot(x_ref[...].astype(BF), w_ref[...], preferred_element_type=F32)
    for o_ref in o_refs:
        o_ref[...] = acc.astype(o_ref.dtype)


def _proj_plain(x, w, out_dtypes, *, tm, tn, name):
    m, k = x.shape
    n = w.shape[1]
    return pl.pallas_call(
        _proj_plain_kernel,
        grid=(m // tm, n // tn),
        in_specs=[pl.BlockSpec((tm, k), lambda i, j: (i, 0)),
                  pl.BlockSpec((k, tn), lambda i, j: (0, j))],
        out_specs=[pl.BlockSpec((tm, tn), lambda i, j: (i, j)) for _ in out_dtypes],
        out_shape=[jax.ShapeDtypeStruct((m, n), dt) for dt in out_dtypes],
        compiler_params=_cparams(("parallel", "parallel")),
        name=name,
    )(x, w)


def _ret_qk_kernel(x_ref, w_ref, cos_ref, sin_ref, o_ref, *, n_q_blocks, k_scale):
    acc = jnp.dot(x_ref[...].astype(BF), w_ref[...], preferred_element_type=F32)
    half = RET_DK // 2
    c = cos_ref[...]
    s = sin_ref[...]
    scale = jnp.where(pl.program_id(1) >= n_q_blocks, k_scale, 1.0).astype(F32)
    for h in range(acc.shape[1] // RET_DK):
        x1 = acc[:, h * RET_DK:h * RET_DK + half]
        x2 = acc[:, h * RET_DK + half:(h + 1) * RET_DK]
        o_ref[:, h * RET_DK:h * RET_DK + half] = ((x1 * c - x2 * s) * scale).astype(o_ref.dtype)
        o_ref[:, h * RET_DK + half:(h + 1) * RET_DK] = ((x2 * c + x1 * s) * scale).astype(o_ref.dtype)


def _ret_qk_proj(x, w_qk, cos, sin, out_dtype, *, tm, tn, name):
    m, k = x.shape
    n = w_qk.shape[1]
    nb = n // tn
    assert tn % RET_DK == 0 and nb % 2 == 0
    ntab = cos.shape[0] // tm
    kern = functools.partial(_ret_qk_kernel, n_q_blocks=nb // 2, k_scale=RET_DK ** -0.5)
    return pl.pallas_call(
        kern,
        grid=(m // tm, nb),
        in_specs=[pl.BlockSpec((tm, k), lambda i, j: (i, 0)),
                  pl.BlockSpec((k, tn), lambda i, j: (0, j)),
                  pl.BlockSpec((tm, RET_DK // 2), lambda i, j: (i % ntab, 0)),
                  pl.BlockSpec((tm, RET_DK // 2), lambda i, j: (i % ntab, 0))],
        out_specs=pl.BlockSpec((tm, tn), lambda i, j: (i, j)),
        out_shape=jax.ShapeDtypeStruct((m, n), out_dtype),
        compiler_params=_cparams(("parallel", "parallel")),
        name=name,
    )(x, w_qk, cos, sin)


def _shift_rope_kernel(x_ref, w_ref, c_ref, a_ref, b_ref, *o_refs, scale, shift):
    acc = jnp.dot(x_ref[...].astype(BF), w_ref[...], preferred_element_type=F32)
    tab_tiles = c_ref.shape[1] // LANES
    for t in range(acc.shape[1] // LANES):
        tt = t % tab_tiles
        cols = slice(tt * LANES, (tt + 1) * LANES)
        xs = acc[:, t * LANES:(t + 1) * LANES]
        up = pltpu.roll(xs, LANES - shift, axis=1)
        dn = pltpu.roll(xs, shift, axis=1)
        r = xs * c_ref[:, cols] + up * a_ref[:, cols] + dn * b_ref[:, cols]
        if scale != 1.0:
            r = r * scale
        for o_ref in o_refs:
            o_ref[:, t * LANES:(t + 1) * LANES] = r.astype(o_ref.dtype)


def _shift_rope_proj(x, w, tabs, out_dtypes, *, scale, shift, tm, tn, name):
    m, k = x.shape
    n = w.shape[1]
    ntab = tabs[0].shape[0] // tm
    tw = min(tabs[0].shape[1], tn)
    ntab_cols = tabs[0].shape[1] // tw
    kern = functools.partial(_shift_rope_kernel, scale=scale, shift=shift)
    tab_spec = pl.BlockSpec((tm, tw), lambda i, j: (i % ntab, j % ntab_cols))
    return pl.pallas_call(
        kern,
        grid=(m // tm, n // tn),
        in_specs=[pl.BlockSpec((tm, k), lambda i, j: (i, 0)),
                  pl.BlockSpec((k, tn), lambda i, j: (0, j)),
                  tab_spec, tab_spec, tab_spec],
        out_specs=[pl.BlockSpec((tm, tn), lambda i, j: (i, j)) for _ in out_dtypes],
        out_shape=[jax.ShapeDtypeStruct((m, n), dt) for dt in out_dtypes],
        compiler_params=_cparams(("parallel", "parallel")),
        name=name,
    )(x, w, *tabs)


def _resident(block_shape, index_map):
    return pl.BlockSpec(block_shape, index_map, pipeline_mode=pl.Buffered(1))


def _ret_proj_kernel(x_ref, w_ref, cos_ref, sin_ref, qk_ref, v_ref, g_ref, *, heads, tn):
    xb = x_ref[...].astype(BF)
    hk, hv = heads * RET_DK, heads * RET_DV
    half = RET_DK // 2
    c = cos_ref[...]
    s = sin_ref[...]
    for h in range(2 * heads):
        acc = jnp.dot(xb, w_ref[:, h * RET_DK:(h + 1) * RET_DK], preferred_element_type=F32)
        x1 = acc[:, :half]
        x2 = acc[:, half:]
        r1 = x1 * c - x2 * s
        r2 = x2 * c + x1 * s
        if h >= heads:
            r1 = r1 * RET_DK ** -0.5
            r2 = r2 * RET_DK ** -0.5
        qk_ref[:, h * RET_DK:h * RET_DK + half] = r1.astype(qk_ref.dtype)
        qk_ref[:, h * RET_DK + half:(h + 1) * RET_DK] = r2.astype(qk_ref.dtype)
    for t in range(hv // tn):
        cols = slice(t * tn, (t + 1) * tn)
        v_ref[:, cols] = jnp.dot(xb, w_ref[:, 2 * hk + t * tn:2 * hk + (t + 1) * tn],
                                 preferred_element_type=F32).astype(v_ref.dtype)
    for t in range(hv // tn):
        cols = slice(t * tn, (t + 1) * tn)
        g_ref[:, cols] = jnp.dot(xb, w_ref[:, 2 * hk + hv + t * tn:2 * hk + hv + (t + 1) * tn],
                                 preferred_element_type=F32).astype(g_ref.dtype)


def _ret_proj(x, w_all, cos, sin, *, heads, tm, name):
    m, k = x.shape
    hk, hv = heads * RET_DK, heads * RET_DV
    ntab = cos.shape[0] // tm
    kern = functools.partial(_ret_proj_kernel, heads=heads, tn=512)
    tab_spec = pl.BlockSpec((tm, RET_DK // 2), lambda i: (i % ntab, 0))
    return pl.pallas_call(
        kern,
        grid=(m // tm,),
        in_specs=[pl.BlockSpec((tm, k), lambda i: (i, 0)),
                  _resident((k, 2 * hk + 2 * hv), lambda i: (0, 0)),
                  tab_spec, tab_spec],
        out_specs=[pl.BlockSpec((tm, 2 * hk), lambda i: (i, 0)),
                   pl.BlockSpec((tm, hv), lambda i: (i, 0)),
                   pl.BlockSpec((tm, hv), lambda i: (i, 0))],
        out_shape=[jax.ShapeDtypeStruct((m, 2 * hk), BF),
                   jax.ShapeDtypeStruct((m, hv), BF),
                   jax.ShapeDtypeStruct((m, hv), F32)],
        compiler_params=_cparams(("parallel",)),
        name=name,
    )(x, w_all, cos, sin)


def _diff_proj_kernel(x_ref, w_ref, c_ref, a_ref, b_ref, kf_in_ref, vf_in_ref,
                      q_ref, kf_ref, kb_ref, vf_ref, vb_ref, *, hd, q_scale, shift, tn):
    del kf_in_ref, vf_in_ref
    xb = x_ref[...].astype(BF)
    c = c_ref[...]
    a = a_ref[...]
    b = b_ref[...]

    def rope(acc, scale, col0, o_refs):
        for t in range(acc.shape[1] // LANES):
            xs = acc[:, t * LANES:(t + 1) * LANES]
            up = pltpu.roll(xs, LANES - shift, axis=1)
            dn = pltpu.roll(xs, shift, axis=1)
            r = xs * c + up * a + dn * b
            if scale != 1.0:
                r = r * scale
            for o_ref in o_refs:
                o_ref[:, col0 + t * LANES:col0 + (t + 1) * LANES] = r.astype(o_ref.dtype)

    for t in range(hd // tn):
        acc = jnp.dot(xb, w_ref[:, t * tn:(t + 1) * tn], preferred_element_type=F32)
        rope(acc, q_scale, t * tn, (q_ref,))
    for t in range(hd // tn):
        acc = jnp.dot(xb, w_ref[:, hd + t * tn:hd + (t + 1) * tn], preferred_element_type=F32)
        rope(acc, 1.0, t * tn, (kf_ref, kb_ref))
    for t in range(hd // tn):
        cols = slice(t * tn, (t + 1) * tn)
        acc = jnp.dot(xb, w_ref[:, 2 * hd + t * tn:2 * hd + (t + 1) * tn], preferred_element_type=F32)
        vf_ref[:, cols] = acc
        vb_ref[:, cols] = acc.astype(BF)


def _diff_proj(x, w_all, tabs, kf_all, vf_all, layer, *, q_scale, shift, tm, name):
    m, k = x.shape
    hd = w_all.shape[1] // 3
    ntab = tabs[0].shape[0] // tm
    kern = functools.partial(_diff_proj_kernel, hd=hd, q_scale=q_scale, shift=shift, tn=512)
    tab_spec = pl.BlockSpec((tm, LANES), lambda i: (i % ntab, 0))
    row_spec = pl.BlockSpec((tm, hd), lambda i: (i, 0))
    layer_spec = pl.BlockSpec((None, tm, hd), lambda i: (layer, i, 0))
    return pl.pallas_call(
        kern,
        grid=(m // tm,),
        in_specs=[pl.BlockSpec((tm, k), lambda i: (i, 0)),
                  _resident((k, 3 * hd), lambda i: (0, 0)),
                  tab_spec, tab_spec, tab_spec,
                  pl.BlockSpec(memory_space=pl.ANY), pl.BlockSpec(memory_space=pl.ANY)],
        out_specs=[row_spec, layer_spec, row_spec, layer_spec, row_spec],
        out_shape=[jax.ShapeDtypeStruct((m, hd), BF), jax.ShapeDtypeStruct(kf_all.shape, F32),
                   jax.ShapeDtypeStruct((m, hd), BF), jax.ShapeDtypeStruct(vf_all.shape, F32),
                   jax.ShapeDtypeStruct((m, hd), BF)],
        input_output_aliases={5: 1, 6: 3},
        compiler_params=_cparams(("parallel",)),
        name=name,
    )(x, w_all, *tabs, kf_all, vf_all)


def _out_norm_kernel(a_ref, w_ref, x_ref, g_ref, b_ref, o_ref, ob_ref, *, alpha, rb):
    w = w_ref[...]
    for r in range(a_ref.shape[0] // rb):
        rows = pl.ds(r * rb, rb)
        sub = jnp.dot(a_ref[rows, :].astype(BF), w, preferred_element_type=F32)
        y = _layer_norm(alpha * x_ref[rows, :] + sub, g_ref[...], b_ref[...])
        o_ref[rows, :] = y
        ob_ref[rows, :] = y.astype(BF)


def _out_norm(a, w, x, g, b, *, alpha, tm, name):
    m, ka = a.shape
    d = w.shape[1]
    kern = functools.partial(_out_norm_kernel, alpha=alpha, rb=min(tm, 256))
    return pl.pallas_call(
        kern,
        grid=(m // tm,),
        in_specs=[pl.BlockSpec((tm, ka), lambda i: (i, 0)),
                  pl.BlockSpec((ka, d), lambda i: (0, 0)),
                  pl.BlockSpec((tm, d), lambda i: (i, 0)),
                  pl.BlockSpec((1, d), lambda i: (0, 0)),
                  pl.BlockSpec((1, d), lambda i: (0, 0))],
        out_specs=[pl.BlockSpec((tm, d), lambda i: (i, 0)),
                   pl.BlockSpec((tm, d), lambda i: (i, 0))],
        out_shape=[jax.ShapeDtypeStruct((m, d), F32), jax.ShapeDtypeStruct((m, d), BF)],
        compiler_params=_cparams(("parallel",)),
        name=name,
    )(a, w, x, g, b)


def _ffn_kernel(x_ref, xb_ref, wg_ref, wu_ref, wd_ref, g_ref, b_ref, o_ref, ob_ref, *, alpha, th):
    xb = xb_ref[...]
    hidden = wd_ref.shape[0]
    hs = []
    for c in range(hidden // th):
        cols = slice(c * th, (c + 1) * th)
        gate = jnp.dot(xb, wg_ref[:, cols], preferred_element_type=F32)
        up = jnp.dot(xb, wu_ref[:, cols], preferred_element_type=F32)
        hs.append((gate * _sigmoid(gate) * up).astype(BF))
    h = jnp.concatenate(hs, axis=1)
    sub = jnp.dot(h, wd_ref[...], preferred_element_type=F32)
    y = _layer_norm(alpha * x_ref[...] + sub, g_ref[...], b_ref[...])
    o_ref[...] = y
    ob_ref[...] = y.astype(BF)


def _ffn(x, xb, w_in, w_down, g, b, *, alpha, tm, th, name):
    m, d = x.shape
    hidden = w_down.shape[0]
    kern = functools.partial(_ffn_kernel, alpha=alpha, th=th)
    return pl.pallas_call(
        kern,
        grid=(m // tm,),
        in_specs=[pl.BlockSpec((tm, d), lambda i: (i, 0)),
                  pl.BlockSpec((tm, d), lambda i: (i, 0)),
                  _resident((d, hidden), lambda i: (0, 0)),
                  _resident((d, hidden), lambda i: (0, 1)),
                  _resident((hidden, d), lambda i: (0, 0)),
                  pl.BlockSpec((1, d), lambda i: (0, 0)),
                  pl.BlockSpec((1, d), lambda i: (0, 0))],
        out_specs=[pl.BlockSpec((tm, d), lambda i: (i, 0)),
                   pl.BlockSpec((tm, d), lambda i: (i, 0))],
        out_shape=[jax.ShapeDtypeStruct((m, d), F32), jax.ShapeDtypeStruct((m, d), BF)],
        compiler_params=_cparams(("parallel",)),
        name=name,
    )(x, xb, w_in, w_in, w_down, g, b)


def _ret_chunk_kernel(lg_ref, q_ref, k_ref, v_ref, g_ref, gn_ref, o_ref, st_ref, s_ref, decay_ref, *, chunk):
    c = pl.program_id(2)
    half = RET_DK // 2
    lg_row = lg_ref[0]
    lg = lg_row[:, :1]

    @pl.when(c == 0)
    def _():
        s_ref[...] = jnp.zeros_like(s_ref)
        ii = lax.broadcasted_iota(jnp.int32, (chunk, chunk), 0)
        jj = lax.broadcasted_iota(jnp.int32, (chunk, chunk), 1)
        rel = (ii - jj).astype(F32)
        decay_ref[...] = jnp.where(rel >= 0, jnp.exp(lg_row * jnp.maximum(rel, 0.0)), 0.0)

    idx = lax.broadcasted_iota(jnp.int32, (chunk, 1), 0).astype(F32)
    q_decay = jnp.exp(lg * (idx + 1.0))
    k_decay = jnp.exp(lg * (chunk - 1.0 - idx))
    chunk_decay = jnp.exp(lg * float(chunk))

    q = q_ref[...]
    k = k_ref[...]
    v = v_ref[...]
    s_old = s_ref[...]
    scores = lax.dot_general(q, k, (((1,), (1,)), ((), ())), preferred_element_type=F32) * decay_ref[...]
    inner = jnp.dot(scores.astype(BF), v, preferred_element_type=F32)
    cross = jnp.dot(q, s_old.astype(BF), preferred_element_type=F32) * q_decay
    kd_t = (k.astype(F32) * k_decay).T.astype(BF)
    s_ref[...] = chunk_decay * s_old + jnp.dot(kd_t, v, preferred_element_type=F32)

    o = inner + cross
    mu = jnp.mean(o, axis=-1, keepdims=True)
    d = o - mu
    var = jnp.mean(d * d, axis=-1, keepdims=True)
    on = d * lax.rsqrt(var + LN_EPS) * gn_ref[...]
    gt = g_ref[...].astype(F32)
    o_ref[...] = (gt * _sigmoid(gt) * on).astype(o_ref.dtype)

    @pl.when(c == pl.num_programs(2) - 1)
    def _():
        st_ref[0, 0, :, :RET_DV] = s_ref[:half, :]
        st_ref[0, 0, :, RET_DV:] = s_ref[half:, :]


def _ret_chunk(lg_tab, qk, v, g, gn_g, *, batch, heads, chunk, name):
    m = qk.shape[0]
    t = m // batch
    nc = t // chunk
    kern = functools.partial(_ret_chunk_kernel, chunk=chunk)
    return pl.pallas_call(
        kern,
        grid=(batch, heads, nc),
        in_specs=[pl.BlockSpec((1, 1, chunk), lambda b, h, c: (h, 0, 0)),
                  pl.BlockSpec((chunk, RET_DK), lambda b, h, c: (b * nc + c, h)),
                  pl.BlockSpec((chunk, RET_DK), lambda b, h, c: (b * nc + c, heads + h)),
                  pl.BlockSpec((chunk, RET_DV), lambda b, h, c: (b * nc + c, h)),
                  pl.BlockSpec((chunk, RET_DV), lambda b, h, c: (b * nc + c, h)),
                  pl.BlockSpec((1, RET_DV), lambda b, h, c: (0, h))],
        out_specs=[pl.BlockSpec((chunk, RET_DV), lambda b, h, c: (b * nc + c, h)),
                   pl.BlockSpec((1, 1, RET_DK // 2, 2 * RET_DV), lambda b, h, c: (b, h, 0, 0))],
        out_shape=[jax.ShapeDtypeStruct((m, heads * RET_DV), BF),
                   jax.ShapeDtypeStruct((batch, heads, RET_DK // 2, 2 * RET_DV), F32)],
        scratch_shapes=[pltpu.VMEM((RET_DK, RET_DV), F32), pltpu.VMEM((chunk, chunk), F32)],
        compiler_params=_cparams(("parallel", "parallel", "arbitrary")),
        name=name,
    )(lg_tab, qk, qk, v, g, gn_g)


def _ret_step_kernel(lg_ref, q_ref, k_ref, v_ref, g_ref, gn_ref, s0_ref, st_in_ref, o_ref, st_ref, *, heads):
    del st_in_ref
    qrow = q_ref[0]
    krow = k_ref[0]
    vrow = v_ref[0]
    grow = g_ref[0]
    for h in range(heads):
        gamma = jnp.exp(lg_ref[h][:, :1])
        qh = qrow[:, h * RET_DK:(h + 1) * RET_DK]
        kh = krow[:, h * RET_DK:(h + 1) * RET_DK]
        vh = vrow[:, h * RET_DV:(h + 1) * RET_DV]
        s0 = s0_ref[0, 0, h]
        k_col = jnp.broadcast_to(kh, (8, RET_DK)).T[:, :1]
        st_ref[0, h] = gamma * s0 + k_col * vh
        q8 = jnp.broadcast_to(qh, (8, RET_DK)).astype(BF)
        cross = jnp.dot(q8, s0.astype(BF), preferred_element_type=F32)[:1]
        qk = jnp.sum(qh * kh, axis=-1, keepdims=True)
        o = qk * vh + gamma * cross
        mu = jnp.mean(o, axis=-1, keepdims=True)
        d = o - mu
        var = jnp.mean(d * d, axis=-1, keepdims=True)
        on = d * lax.rsqrt(var + LN_EPS) * gn_ref[:, h * RET_DV:(h + 1) * RET_DV]
        gt = grow[:, h * RET_DV:(h + 1) * RET_DV]
        o_ref[0, :, h * RET_DV:(h + 1) * RET_DV] = (gt * _sigmoid(gt) * on).astype(o_ref.dtype)


def _ret_step(lg_tab, qk, v, g, gn_g, state_all, st_all, layer, *, heads, name):
    bs = qk.shape[0]
    hk = heads * RET_DK
    hv = heads * RET_DV
    q3 = qk.reshape(bs, 1, 2 * hk)
    v3 = v.reshape(bs, 1, hv)
    g3 = g.reshape(bs, 1, hv)
    kern = functools.partial(_ret_step_kernel, heads=heads)
    out, st = pl.pallas_call(
        kern,
        grid=(bs,),
        in_specs=[pl.BlockSpec(lg_tab.shape, lambda b: (0, 0, 0)),
                  pl.BlockSpec((1, 1, hk), lambda b: (b, 0, 0)),
                  pl.BlockSpec((1, 1, hk), lambda b: (b, 0, 1)),
                  pl.BlockSpec((1, 1, hv), lambda b: (b, 0, 0)),
                  pl.BlockSpec((1, 1, hv), lambda b: (b, 0, 0)),
                  pl.BlockSpec((1, hv), lambda b: (0, 0)),
                  pl.BlockSpec((1, 1, heads, RET_DK, RET_DV), lambda b: (layer, b, 0, 0, 0)),
                  pl.BlockSpec(memory_space=pl.ANY)],
        out_specs=[pl.BlockSpec((1, 1, hv), lambda b: (b, 0, 0)),
                   pl.BlockSpec((None, 1, heads, RET_DK, RET_DV), lambda b: (layer, b, 0, 0, 0))],
        out_shape=[jax.ShapeDtypeStruct((bs, 1, hv), BF),
                   jax.ShapeDtypeStruct(st_all.shape, F32)],
        input_output_aliases={7: 1},
        compiler_params=_cparams(("parallel",)),
        name=name,
    )(lg_tab, q3, q3, v3, g3, gn_g, state_all, st_all)
    return out.reshape(bs, hv), st


def _diff_lambda(lq1_ref, lk1_ref, lq2_ref, lk2_ref, lam_init):
    a = jnp.sum(lq1_ref[...] * lk1_ref[...], axis=-1, keepdims=True)
    b = jnp.sum(lq2_ref[...] * lk2_ref[...], axis=-1, keepdims=True)
    return jnp.exp(a) - jnp.exp(b) + lam_init


def _diff_flash_kernel(qi_ref, kj_ref, q_ref, k_ref, v_ref, lq1_ref, lk1_ref, lq2_ref, lk2_ref, sg_ref,
                       o_ref, q1_ref, q2_ref, m1_ref, m2_ref, l1_ref, l2_ref, a1_ref, a2_ref,
                       *, tq, tk, rb, lam_init):
    step = pl.program_id(2)
    qi = qi_ref[step]
    kj = kj_ref[step]
    last_kj = ((qi + 1) * tq - 1) // tk
    comps = ((q1_ref, m1_ref, l1_ref, a1_ref), (q2_ref, m2_ref, l2_ref, a2_ref))

    @pl.when(kj == 0)
    def _():
        q = q_ref[...]
        lane = lax.broadcasted_iota(jnp.int32, q.shape, 1)
        zero = jnp.zeros_like(q)
        q1_ref[...] = jnp.where(lane < DIFF_D, q, zero)
        q2_ref[...] = jnp.where(lane >= DIFF_D, q, zero)
        for _, m_ref, l_ref, a_ref in comps:
            m_ref[...] = jnp.full_like(m_ref, -jnp.inf)
            l_ref[...] = jnp.zeros_like(l_ref)
            a_ref[...] = jnp.zeros_like(a_ref)

    def update(masked):
        k = k_ref[...]
        v = v_ref[...]
        v_ext = jnp.concatenate([v, jnp.ones_like(v)], axis=1)
        tri = (lax.broadcasted_iota(jnp.int32, (rb, rb), 1) <= lax.broadcasted_iota(jnp.int32, (rb, rb), 0))
        for qc_ref, m_ref, l_ref, a_ref in comps:
            for r in range(tq // rb):
                rows = pl.ds(r * rb, rb)
                w = min((r + 1) * rb, tk) if (masked and tq == tk) else tk
                s = lax.dot_general(qc_ref[rows, :], k[:w], (((1,), (1,)), ((), ())),
                                    preferred_element_type=F32)
                if masked and tq == tk:
                    lo = r * rb
                    s_diag = jnp.where(tri, s[:, lo:], -jnp.inf)
                    s = s_diag if r == 0 else jnp.concatenate([s[:, :lo], s_diag], axis=1)
                elif masked:
                    row = lax.broadcasted_iota(jnp.int32, (rb, w), 0) + (qi * tq + r * rb)
                    col = lax.broadcasted_iota(jnp.int32, (rb, w), 1) + kj * tk
                    s = jnp.where(col <= row, s, -jnp.inf)
                m_prev = m_ref[rows, :]
                m_new = jnp.maximum(m_prev, jnp.max(s, axis=-1, keepdims=True))
                alpha = jnp.exp2(m_prev - m_new)
                p = jnp.exp2(s - jnp.tile(m_new, (1, w // LANES)))
                pv = jnp.dot(p.astype(BF), v_ext[:w], preferred_element_type=F32)
                a_ref[rows, :] = alpha * a_ref[rows, :] + pv[:, :LANES]
                l_ref[rows, :] = alpha * l_ref[rows, :] + pv[:, LANES:]
                m_ref[rows, :] = m_new

    straddles = (kj + 1) * tk - 1 > qi * tq

    @pl.when(straddles)
    def _():
        update(True)

    @pl.when(jnp.logical_not(straddles))
    def _():
        update(False)

    @pl.when(kj == last_kj)
    def _():
        lam = _diff_lambda(lq1_ref, lk1_ref, lq2_ref, lk2_ref, lam_init)
        o = a1_ref[...] / l1_ref[...] - lam * (a2_ref[...] / l2_ref[...])
        o = o * lax.rsqrt(jnp.mean(o * o, axis=-1, keepdims=True) + LN_EPS)
        o_ref[...] = (o * sg_ref[...] * (1.0 - lam_init)).astype(o_ref.dtype)


def _diff_flash(q, k, v, lq1, lk1, lq2, lk2, subln_g, *, batch, heads, tq, tk, rb, lam_init, name):
    m = q.shape[0]
    t = m // batch
    nq, nk = t // tq, t // tk
    qi_list, kj_list = [], []
    for i in range(nq):
        for j in range(((i + 1) * tq - 1) // tk + 1):
            qi_list.append(i)
            kj_list.append(j)
    qi_tab = jnp.asarray(np.array(qi_list, np.int32))
    kj_tab = jnp.asarray(np.array(kj_list, np.int32))
    hd = 2 * DIFF_D
    assert hd == LANES
    kern = functools.partial(_diff_flash_kernel, tq=tq, tk=tk, rb=rb, lam_init=lam_init)
    vec_spec = pl.BlockSpec((1, DIFF_D), lambda b, h, s, qi, kj: (0, 0))
    grid_spec = pltpu.PrefetchScalarGridSpec(
        num_scalar_prefetch=2,
        grid=(batch, heads, len(qi_list)),
        in_specs=[pl.BlockSpec((tq, hd), lambda b, h, s, qi, kj: (b * nq + qi[s], h)),
                  pl.BlockSpec((tk, hd), lambda b, h, s, qi, kj: (b * nk + kj[s], h)),
                  pl.BlockSpec((tk, hd), lambda b, h, s, qi, kj: (b * nk + kj[s], h)),
                  vec_spec, vec_spec, vec_spec, vec_spec,
                  pl.BlockSpec((1, hd), lambda b, h, s, qi, kj: (0, h))],
        out_specs=pl.BlockSpec((tq, hd), lambda b, h, s, qi, kj: (b * nq + qi[s], h)),
        scratch_shapes=[pltpu.VMEM((tq, hd), BF), pltpu.VMEM((tq, hd), BF)]
                       + [pltpu.VMEM((tq, hd), F32) for _ in range(6)],
    )
    return pl.pallas_call(
        kern,
        grid_spec=grid_spec,
        out_shape=jax.ShapeDtypeStruct((m, heads * hd), BF),
        compiler_params=_cparams(("parallel", "parallel", "arbitrary")),
        name=name,
    )(qi_tab, kj_tab, q, k, v, lq1, lk1, lq2, lk2, subln_g)


def _diff_decode_kernel(pt_ref, q_ref, kn_ref, vn_ref, lq1_ref, lk1_ref, lq2_ref, lk2_ref, sg_ref, *rest,
                        heads, n_fetch, lam_init):
    del pt_ref
    k_refs = rest[:n_fetch]
    v_refs = rest[n_fetch:2 * n_fetch]
    o_ref, m_ref, l_ref, acc_ref = rest[2 * n_fetch:]
    p_idx = pl.program_id(1)
    rows = 2 * heads

    @pl.when(p_idx == 0)
    def _():
        m_ref[...] = jnp.full_like(m_ref, -jnp.inf)
        l_ref[...] = jnp.zeros_like(l_ref)
        acc_ref[...] = jnp.zeros_like(acc_ref)

    q8 = q_ref[0]
    lane = lax.broadcasted_iota(jnp.int32, q8.shape, 1)
    q16 = jnp.concatenate([jnp.where(lane < DIFF_D, q8, 0.0),
                           jnp.where(lane >= DIFF_D, q8, 0.0)], axis=0)
    q16b = q16.astype(BF)

    ncol = k_refs[0].shape[1]
    r_head = lax.broadcasted_iota(jnp.int32, (rows, ncol), 0) % heads
    c_head = lax.broadcasted_iota(jnp.int32, (rows, ncol), 1) % heads
    same_head = r_head == c_head
    s_pages = []
    for k_ref in k_refs:
        s = lax.dot_general(q16b, k_ref[0].astype(BF), (((1,), (1,)), ((), ())),
                            preferred_element_type=F32)
        s_pages.append(jnp.where(same_head, s, -jnp.inf))
    m_prev = m_ref[...]
    m_new = m_prev
    for s in s_pages:
        m_new = jnp.maximum(m_new, jnp.max(s, axis=-1, keepdims=True))
    alpha = jnp.exp(m_prev - m_new)
    l_new = alpha * l_ref[...]
    acc = alpha * acc_ref[...]
    for s, v_ref in zip(s_pages, v_refs):
        p = jnp.exp(s - m_new)
        l_new = l_new + jnp.sum(p, axis=-1, keepdims=True)
        acc = acc + jnp.dot(p.astype(BF), v_ref[0].astype(BF), preferred_element_type=F32)
    l_ref[...] = l_new
    acc_ref[...] = acc
    m_ref[...] = m_new

    @pl.when(p_idx == pl.num_programs(1) - 1)
    def _():
        kn = kn_ref[0]
        vn = vn_ref[0]
        kn2 = jnp.concatenate([kn, kn], axis=0)
        vn2 = jnp.concatenate([vn, vn], axis=0)
        s_n = jnp.sum(q16 * kn2, axis=-1, keepdims=True)
        m_fin = jnp.maximum(m_new, s_n)
        beta = jnp.exp(m_new - m_fin)
        p_n = jnp.exp(s_n - m_fin)
        l_fin = beta * l_new + p_n
        acc_fin = beta * acc + p_n * vn2
        lam = _diff_lambda(lq1_ref, lk1_ref, lq2_ref, lk2_ref, lam_init)
        o = acc_fin[:heads] / l_fin[:heads] - lam * (acc_fin[heads:] / l_fin[heads:])
        o = o * lax.rsqrt(jnp.mean(o * o, axis=-1, keepdims=True) + LN_EPS)
        o_ref[0] = (o * sg_ref[...] * (1.0 - lam_init)).astype(o_ref.dtype)


def _diff_decode(page_table, q, kn, vn, lq1, lk1, lq2, lk2, subln_g, cache_k, cache_v, layer, *,
                 heads, lam_init, name):
    bs, n_pages = page_table.shape
    hd = 2 * DIFF_D
    n_layers, n_phys = cache_k.shape[:2]
    ck = cache_k.reshape(n_layers * n_phys, PAGE_SIZE * heads, hd)
    cv = cache_v.reshape(n_layers * n_phys, PAGE_SIZE * heads, hd)
    q3 = q.reshape(bs, heads, hd)
    kn3 = kn.reshape(bs, heads, hd)
    vn3 = vn.reshape(bs, heads, hd)
    sg = subln_g.reshape(heads, hd)
    pt = page_table.reshape(-1)
    n_fetch = DECODE_PAGES if n_pages % DECODE_PAGES == 0 else 1
    base = layer * n_phys
    kern = functools.partial(_diff_decode_kernel, heads=heads, n_fetch=n_fetch, lam_init=lam_init)
    row_spec = pl.BlockSpec((1, heads, hd), lambda b, p, pt: (b, 0, 0))
    vec_spec = pl.BlockSpec((1, DIFF_D), lambda b, p, pt: (0, 0))

    def page_spec(r):
        return pl.BlockSpec((1, PAGE_SIZE * heads, hd),
                            lambda b, p, pt: (base + pt[b * n_pages + p * n_fetch + r], 0, 0))

    page_specs = [page_spec(r) for r in range(n_fetch)]
    grid_spec = pltpu.PrefetchScalarGridSpec(
        num_scalar_prefetch=1,
        grid=(bs, n_pages // n_fetch),
        in_specs=[row_spec, row_spec, row_spec, vec_spec, vec_spec, vec_spec, vec_spec,
                  pl.BlockSpec((heads, hd), lambda b, p, pt: (0, 0))] + page_specs + page_specs,
        out_specs=pl.BlockSpec((1, heads, hd), lambda b, p, pt: (b, 0, 0)),
        scratch_shapes=[pltpu.VMEM((2 * heads, 1), F32),
                        pltpu.VMEM((2 * heads, 1), F32),
                        pltpu.VMEM((2 * heads, hd), F32)],
    )
    out = pl.pallas_call(
        kern,
        grid_spec=grid_spec,
        out_shape=jax.ShapeDtypeStruct((bs, heads, hd), BF),
        compiler_params=_cparams(("parallel", "arbitrary")),
        name=name,
    )(pt, q3, kn3, vn3, lq1, lk1, lq2, lk2, sg, *([ck] * n_fetch), *([cv] * n_fetch))
    return out.reshape(bs, heads * hd)


def _ret_tables(pos):
    inv = 1.0 / (RET_ROPE_BASE ** jnp.linspace(0.0, 1.0, RET_DK // 2, dtype=F32))
    ang = pos.astype(F32)[:, None] * inv[None, :]
    return jnp.cos(ang), jnp.sin(ang)


def _ret_tables_interleaved(pos, heads):
    cos, sin = _ret_tables(pos)
    zero = jnp.zeros_like(sin)
    c = jnp.stack([cos, cos], axis=-1).reshape(pos.shape[0], RET_DK)
    a = jnp.stack([-sin, zero], axis=-1).reshape(pos.shape[0], RET_DK)
    b = jnp.stack([zero, sin], axis=-1).reshape(pos.shape[0], RET_DK)
    k_scale = RET_DK ** -0.5
    return tuple(jnp.concatenate([jnp.tile(tab, (1, heads)), jnp.tile(tab * k_scale, (1, heads))], axis=1)
                 for tab in (c, a, b))


def _diff_tables(pos):
    half = DIFF_ROT // 2
    inv = 1.0 / (ROPE_THETA ** (jnp.arange(half, dtype=F32) * 2.0 / DIFF_ROT))
    ang = pos.astype(F32)[:, None] * inv[None, :]
    cos, sin = jnp.cos(ang), jnp.sin(ang)
    n = pos.shape[0]
    ones = jnp.ones((n, DIFF_D - DIFF_ROT), F32)
    zeros_rest = jnp.zeros((n, DIFF_D - DIFF_ROT), F32)
    zeros_h = jnp.zeros((n, half), F32)
    c = jnp.concatenate([cos, cos, ones], axis=1)
    a = jnp.concatenate([-sin, zeros_h, zeros_rest], axis=1)
    b = jnp.concatenate([zeros_h, sin, zeros_rest], axis=1)
    rep = LANES // DIFF_D
    return tuple(jnp.tile(tab, (1, rep)) for tab in (c, a, b))


def _ret_split_weights(w_in, heads):
    hk, hv = heads * RET_DK, heads * RET_DV
    perm = np.concatenate([np.arange(0, RET_DK, 2), np.arange(1, RET_DK, 2)])
    cols = np.concatenate([h * RET_DK + perm for h in range(heads)])
    w_q = w_in[:, :hk][:, cols]
    w_k = w_in[:, hk:2 * hk][:, cols]
    w_qk = jnp.concatenate([w_q, w_k], axis=1).astype(BF)
    w_v = w_in[:, 2 * hk:2 * hk + hv].astype(BF)
    w_g = w_in[:, 2 * hk + hv:].astype(BF)
    return w_qk, w_v, w_g


def kernel(x_prompt, x_sample, state_ret, cache_k, cache_v, page_table, ret_w_in, ret_gn_g, ret_w_o, diff_w_in, diff_lq1, diff_lk1, diff_lq2, diff_lk2, diff_subln_g, diff_w_o, ffn_w_in, ffn_w_down, ln1_g, ln1_b, ln2_g, ln2_b):
    bp, tp, d_model = x_prompt.shape
    bs, ts, _ = x_sample.shape
    assert ts == 1
    depth = ffn_w_in.shape[0]
    ret_heads = ret_w_o.shape[1] // RET_DV
    diff_heads = diff_w_o.shape[1] // (2 * DIFF_D)
    hd = diff_heads * 2 * DIFF_D
    past_len = page_table.shape[1] * PAGE_SIZE
    alpha = (2.0 * depth) ** 0.25
    mp = bp * tp

    pos_p = jnp.arange(tp)
    pos_s = jnp.broadcast_to(past_len + jnp.arange(ts), (bs,))
    ret_tab_p = _ret_tables(pos_p)
    ret_tab_s = _ret_tables_interleaved(pos_s, ret_heads)
    diff_tab_p = _diff_tables(pos_p)
    diff_tab_s = _diff_tables(pos_s)

    ret_chunk = RET_CHUNK if tp % RET_CHUNK == 0 else tp
    lg = jnp.log(1.0 - 2.0 ** (-5.0 - jnp.arange(ret_heads, dtype=F32)))
    lg_tab = jnp.broadcast_to(lg[:, None, None], (ret_heads, 1, max(ret_chunk, LANES)))

    tm_p = _row_tile(mp, 512)
    tm_o = _row_tile(mp, 512)
    tm_s = bs

    xp = x_prompt.reshape(mp, d_model)
    xs = x_sample.reshape(bs * ts, d_model)
    xp_b, xs_b = xp, xs

    n_diff = diff_w_in.shape[0]
    ret_p, kc_s, vc_s = [], [], []
    ret_s_all = jnp.zeros(state_ret.shape, F32)
    kc_p_all = jnp.zeros((n_diff, mp, hd), F32)
    vc_p_all = jnp.zeros((n_diff, mp, hd), F32)
    for i in range(depth):
        j = i // N_MIXERS
        g1, b1 = ln1_g[i][None, :], ln1_b[i][None, :]
        g2, b2 = ln2_g[i][None, :], ln2_b[i][None, :]
        if i % N_MIXERS == 0:
            w_qk, w_v, w_g = _ret_split_weights(ret_w_in[j], ret_heads)
            w_o = ret_w_o[j].astype(BF)
            gn = ret_gn_g[j][None, :]
            w_all = jnp.concatenate([w_qk, w_v, w_g], axis=1)
            qk, v, g = _ret_proj(xp_b, w_all, *ret_tab_p, heads=ret_heads, tm=tm_p, name=f"ret{j}_proj_p")
            mix_p, st_p = _ret_chunk(lg_tab, qk, v, g, gn, batch=bp, heads=ret_heads, chunk=ret_chunk,
                                     name=f"ret{j}_chunk_p")
            ret_p.append(st_p.reshape(bp, ret_heads, RET_DK, RET_DV))
            w_qk_nat = ret_w_in[j][:, :2 * ret_heads * RET_DK].astype(BF)
            (qk_s,) = _shift_rope_proj(xs_b, w_qk_nat, ret_tab_s, [F32], scale=1.0, shift=1, tm=tm_s, tn=512,
                                       name=f"ret{j}_qk_s")
            (v_s,) = _proj_plain(xs_b, w_v, [F32], tm=tm_s, tn=512, name=f"ret{j}_v_s")
            (g_s,) = _proj_plain(xs_b, w_g, [F32], tm=tm_s, tn=512, name=f"ret{j}_g_s")
            mix_s, ret_s_all = _ret_step(lg_tab, qk_s, v_s, g_s, gn, state_ret, ret_s_all, j, heads=ret_heads,
                                         name=f"ret{j}_step_s")
        else:
            lam_init = 0.8 - 0.6 * math.exp(-0.3 * (i + 1))
            w_in = diff_w_in[j].astype(BF)
            w_q, w_k, w_v = w_in[:, :hd], w_in[:, hd:2 * hd], w_in[:, 2 * hd:]
            w_o = diff_w_o[j].astype(BF)
            lam_args = (diff_lq1[j][None, :], diff_lk1[j][None, :], diff_lq2[j][None, :], diff_lk2[j][None, :])
            sg = diff_subln_g[j][None, :]
            rot = DIFF_ROT // 2
            q, kc_p_all, k_b, vc_p_all, v_b = _diff_proj(xp_b, w_in, diff_tab_p, kc_p_all, vc_p_all, j,
                                                         q_scale=DIFF_D ** -0.5 * math.log2(math.e), shift=rot,
                                                         tm=tm_p, name=f"diff{j}_proj_p")
            tq = FLASH_TQ if tp % FLASH_TQ == 0 else tp
            tk = FLASH_TK if tp % FLASH_TK == 0 else tp
            mix_p = _diff_flash(q, k_b, v_b, *lam_args, sg, batch=bp, heads=diff_heads, tq=tq, tk=tk,
                                rb=min(FLASH_ROWS, tq), lam_init=lam_init, name=f"diff{j}_flash_p")
            (q_s,) = _shift_rope_proj(xs_b, w_q, diff_tab_s, [F32], scale=DIFF_D ** -0.5, shift=rot, tm=tm_s, tn=512,
                                      name=f"diff{j}_q_s")
            (kn,) = _shift_rope_proj(xs_b, w_k, diff_tab_s, [F32], scale=1.0, shift=rot, tm=tm_s, tn=512,
                                     name=f"diff{j}_k_s")
            (vn,) = _proj_plain(xs_b, w_v, [F32], tm=tm_s, tn=512, name=f"diff{j}_v_s")
            mix_s = _diff_decode(page_table, q_s, kn, vn, *lam_args, sg, cache_k, cache_v, j,
                                 heads=diff_heads, lam_init=lam_init, name=f"diff{j}_decode_s")
            kc_s.append(kn.reshape(bs, ts, diff_heads, 2 * DIFF_D))
            vc_s.append(vn.reshape(bs, ts, diff_heads, 2 * DIFF_D))

        xp, xp_b = _out_norm(mix_p, w_o, xp, g1, b1, alpha=alpha, tm=tm_o, name=f"l{i}_mix_norm_p")
        xs, xs_b = _out_norm(mix_s, w_o, xs, g1, b1, alpha=alpha, tm=tm_s, name=f"l{i}_mix_norm_s")
        w_fi = ffn_w_in[i].astype(BF)
        w_fd = ffn_w_down[i].astype(BF)
        xp, xp_b = _ffn(xp, xp_b, w_fi, w_fd, g2, b2, alpha=alpha, tm=_row_tile(mp, FFN_ROWS), th=256,
                        name=f"l{i}_ffn_p")
        xs, xs_b = _ffn(xs, xs_b, w_fi, w_fd, g2, b2, alpha=alpha, tm=tm_s, th=256, name=f"l{i}_ffn_s")

    return (xp.reshape(bp, tp, d_model), xs.reshape(bs, ts, d_model),
            jnp.stack(ret_p),
            kc_p_all.reshape(n_diff, bp, tp, diff_heads, 2 * DIFF_D),
            vc_p_all.reshape(n_diff, bp, tp, diff_heads, 2 * DIFF_D),
            ret_s_all, jnp.stack(kc_s), jnp.stack(vc_s))
```

```python
import functools
import math

import numpy as np
import jax
import jax.numpy as jnp
from jax import lax
from jax.experimental import pallas as pl
from jax.experimental.pallas import tpu as pltpu

BF = jnp.bfloat16
F32 = jnp.float32

RET_DK = 256
RET_DV = 512
RET_ROPE_BASE = 10000.0
DIFF_D = 64
DIFF_ROT = DIFF_D // 4
ROPE_THETA = 500000.0
PAGE_SIZE = 128
LN_EPS = 1e-5
N_MIXERS = 2

LANES = 128
FLASH_TQ = 2048
FLASH_TK = 2048
FLASH_ROWS = 256
FLASH_ROWS_DIAG = 512
RET_CHUNK = 512
FFN_ROWS = 512
DECODE_PAGES = 8
VMEM_LIMIT = 48 * 1024 * 1024


def _cparams(sem):
    return pltpu.CompilerParams(dimension_semantics=sem, vmem_limit_bytes=VMEM_LIMIT)


def _sigmoid(x):
    return 1.0 / (1.0 + jnp.exp(-x))


def _layer_norm(y, g, b):
    mu = jnp.mean(y, axis=-1, keepdims=True)
    d = y - mu
    var = jnp.mean(d * d, axis=-1, keepdims=True)
    return d * lax.rsqrt(var + LN_EPS) * g + b


def _row_tile(m, pref):
    return pref if m % pref == 0 else m


def _proj_plain_kernel(x_ref, w_ref, *o_refs):
    acc = jnp.dot(x_ref[...].astype(BF), w_ref[...], preferred_element_type=F32)
    for o_ref in o_refs:
        o_ref[...] = acc.astype(o_ref.dtype)


def _proj_plain(x, w, out_dtypes, *, tm, tn, name):
    m, k = x.shape
    n = w.shape[1]
    return pl.pallas_call(
        _proj_plain_kernel,
        grid=(m // tm, n // tn),
        in_specs=[pl.BlockSpec((tm, k), lambda i, j: (i, 0)),
                  pl.BlockSpec((k, tn), lambda i, j: (0, j))],
        out_specs=[pl.BlockSpec((tm, tn), lambda i, j: (i, j)) for _ in out_dtypes],
        out_shape=[jax.ShapeDtypeStruct((m, n), dt) for dt in out_dtypes],
        compiler_params=_cparams(("parallel", "parallel")),
        name=name,
    )(x, w)


def _ret_qk_kernel(x_ref, w_ref, cos_ref, sin_ref, o_ref, *, n_q_blocks, k_scale):
    acc = jnp.dot(x_ref[...].astype(BF), w_ref[...], preferred_element_type=F32)
    half = RET_DK // 2
    c = cos_ref[...]
    s = sin_ref[...]
    scale = jnp.where(pl.program_id(1) >= n_q_blocks, k_scale, 1.0).astype(F32)
    for h in range(acc.shape[1] // RET_DK):
        x1 = acc[:, h * RET_DK:h * RET_DK + half]
        x2 = acc[:, h * RET_DK + half:(h + 1) * RET_DK]
        o_ref[:, h * RET_DK:h * RET_DK + half] = ((x1 * c - x2 * s) * scale).astype(o_ref.dtype)
        o_ref[:, h * RET_DK + half:(h + 1) * RET_DK] = ((x2 * c + x1 * s) * scale).astype(o_ref.dtype)


def _ret_qk_proj(x, w_qk, cos, sin, out_dtype, *, tm, tn, name):
    m, k = x.shape
    n = w_qk.shape[1]
    nb = n // tn
    assert tn % RET_DK == 0 and nb % 2 == 0
    ntab = cos.shape[0] // tm
    kern = functools.partial(_ret_qk_kernel, n_q_blocks=nb // 2, k_scale=RET_DK ** -0.5)
    return pl.pallas_call(
        kern,
        grid=(m // tm, nb),
        in_specs=[pl.BlockSpec((tm, k), lambda i, j: (i, 0)),
                  pl.BlockSpec((k, tn), lambda i, j: (0, j)),
                  pl.BlockSpec((tm, RET_DK // 2), lambda i, j: (i % ntab, 0)),
                  pl.BlockSpec((tm, RET_DK // 2), lambda i, j: (i % ntab, 0))],
        out_specs=pl.BlockSpec((tm, tn), lambda i, j: (i, j)),
        out_shape=jax.ShapeDtypeStruct((m, n), out_dtype),
        compiler_params=_cparams(("parallel", "parallel")),
        name=name,
    )(x, w_qk, cos, sin)


def _shift_rope_kernel(x_ref, w_ref, c_ref, a_ref, b_ref, *o_refs, scale, shift):
    acc = jnp.dot(x_ref[...].astype(BF), w_ref[...], preferred_element_type=F32)
    tab_tiles = c_ref.shape[1] // LANES
    for t in range(acc.shape[1] // LANES):
        tt = t % tab_tiles
        cols = slice(tt * LANES, (tt + 1) * LANES)
        xs = acc[:, t * LANES:(t + 1) * LANES]
        up = pltpu.roll(xs, LANES - shift, axis=1)
        dn = pltpu.roll(xs, shift, axis=1)
        r = xs * c_ref[:, cols] + up * a_ref[:, cols] + dn * b_ref[:, cols]
        if scale != 1.0:
            r = r * scale
        for o_ref in o_refs:
            o_ref[:, t * LANES:(t + 1) * LANES] = r.astype(o_ref.dtype)


def _shift_rope_proj(x, w, tabs, out_dtypes, *, scale, shift, tm, tn, name):
    m, k = x.shape
    n = w.shape[1]
    ntab = tabs[0].shape[0] // tm
    tw = min(tabs[0].shape[1], tn)
    ntab_cols = tabs[0].shape[1] // tw
    kern = functools.partial(_shift_rope_kernel, scale=scale, shift=shift)
    tab_spec = pl.BlockSpec((tm, tw), lambda i, j: (i % ntab, j % ntab_cols))
    return pl.pallas_call(
        kern,
        grid=(m // tm, n // tn),
        in_specs=[pl.BlockSpec((tm, k), lambda i, j: (i, 0)),
                  pl.BlockSpec((k, tn), lambda i, j: (0, j)),
                  tab_spec, tab_spec, tab_spec],
        out_specs=[pl.BlockSpec((tm, tn), lambda i, j: (i, j)) for _ in out_dtypes],
        out_shape=[jax.ShapeDtypeStruct((m, n), dt) for dt in out_dtypes],
        compiler_params=_cparams(("parallel", "parallel")),
        name=name,
    )(x, w, *tabs)


def _resident(block_shape, index_map):
    return pl.BlockSpec(block_shape, index_map, pipeline_mode=pl.Buffered(1))


def _ret_proj_kernel(x_ref, w_ref, cos_ref, sin_ref, qk_ref, v_ref, g_ref, *, heads, tn):
    xb = x_ref[...].astype(BF)
    hk, hv = heads * RET_DK, heads * RET_DV
    half = RET_DK // 2
    c = cos_ref[...]
    s = sin_ref[...]
    for h in range(2 * heads):
        acc = jnp.dot(xb, w_ref[:, h * RET_DK:(h + 1) * RET_DK], preferred_element_type=F32)
        x1 = acc[:, :half]
        x2 = acc[:, half:]
        r1 = x1 * c - x2 * s
        r2 = x2 * c + x1 * s
        if h >= heads:
            r1 = r1 * RET_DK ** -0.5
            r2 = r2 * RET_DK ** -0.5
        qk_ref[:, h * RET_DK:h * RET_DK + half] = r1.astype(qk_ref.dtype)
        qk_ref[:, h * RET_DK + half:(h + 1) * RET_DK] = r2.astype(qk_ref.dtype)
    for t in range(hv // tn):
        cols = slice(t * tn, (t + 1) * tn)
        v_ref[:, cols] = jnp.dot(xb, w_ref[:, 2 * hk + t * tn:2 * hk + (t + 1) * tn],
                                 preferred_element_type=F32).astype(v_ref.dtype)
    for t in range(hv // tn):
        cols = slice(t * tn, (t + 1) * tn)
        g_ref[:, cols] = jnp.dot(xb, w_ref[:, 2 * hk + hv + t * tn:2 * hk + hv + (t + 1) * tn],
                                 preferred_element_type=F32).astype(g_ref.dtype)


def _ret_proj(x, w_all, cos, sin, *, heads, tm, name):
    m, k = x.shape
    hk, hv = heads * RET_DK, heads * RET_DV
    ntab = cos.shape[0] // tm
    kern = functools.partial(_ret_proj_kernel, heads=heads, tn=512)
    tab_spec = pl.BlockSpec((tm, RET_DK // 2), lambda i: (i % ntab, 0))
    return pl.pallas_call(
        kern,
        grid=(m // tm,),
        in_specs=[pl.BlockSpec((tm, k), lambda i: (i, 0)),
                  _resident((k, 2 * hk + 2 * hv), lambda i: (0, 0)),
                  tab_spec, tab_spec],
        out_specs=[pl.BlockSpec((tm, 2 * hk), lambda i: (i, 0)),
                   pl.BlockSpec((tm, hv), lambda i: (i, 0)),
                   pl.BlockSpec((tm, hv), lambda i: (i, 0))],
        out_shape=[jax.ShapeDtypeStruct((m, 2 * hk), BF),
                   jax.ShapeDtypeStruct((m, hv), BF),
                   jax.ShapeDtypeStruct((m, hv), F32)],
        compiler_params=_cparams(("parallel",)),
        name=name,
    )(x, w_all, cos, sin)


def _diff_proj_kernel(x_ref, w_ref, c_ref, a_ref, b_ref, kf_in_ref, vf_in_ref,
                      q_ref, kf_ref, kb_ref, vf_ref, vb_ref, *, hd, q_scale, shift, tn):
    del kf_in_ref, vf_in_ref
    xb = x_ref[...].astype(BF)
    c = c_ref[...]
    a = a_ref[...]
    b = b_ref[...]

    def rope(acc, scale, col0, o_refs):
        for t in range(acc.shape[1] // LANES):
            xs = acc[:, t * LANES:(t + 1) * LANES]
            up = pltpu.roll(xs, LANES - shift, axis=1)
            dn = pltpu.roll(xs, shift, axis=1)
            r = xs * c + up * a + dn * b
            if scale != 1.0:
                r = r * scale
            for o_ref in o_refs:
                o_ref[:, col0 + t * LANES:col0 + (t + 1) * LANES] = r.astype(o_ref.dtype)

    for t in range(hd // tn):
        acc = jnp.dot(xb, w_ref[:, t * tn:(t + 1) * tn], preferred_element_type=F32)
        rope(acc, q_scale, t * tn, (q_ref,))
    for t in range(hd // tn):
        acc = jnp.dot(xb, w_ref[:, hd + t * tn:hd + (t + 1) * tn], preferred_element_type=F32)
        rope(acc, 1.0, t * tn, (kf_ref, kb_ref))
    for t in range(hd // tn):
        cols = slice(t * tn, (t + 1) * tn)
        acc = jnp.dot(xb, w_ref[:, 2 * hd + t * tn:2 * hd + (t + 1) * tn], preferred_element_type=F32)
        vf_ref[:, cols] = acc
        vb_ref[:, cols] = acc.astype(BF)


def _diff_proj(x, w_all, tabs, kf_all, vf_all, layer, *, q_scale, shift, tm, name):
    m, k = x.shape
    hd = w_all.shape[1] // 3
    ntab = tabs[0].shape[0] // tm
    kern = functools.partial(_diff_proj_kernel, hd=hd, q_scale=q_scale, shift=shift, tn=512)
    tab_spec = pl.BlockSpec((tm, LANES), lambda i: (i % ntab, 0))
    row_spec = pl.BlockSpec((tm, hd), lambda i: (i, 0))
    layer_spec = pl.BlockSpec((None, tm, hd), lambda i: (layer, i, 0))
    return pl.pallas_call(
        kern,
        grid=(m // tm,),
        in_specs=[pl.BlockSpec((tm, k), lambda i: (i, 0)),
                  _resident((k, 3 * hd), lambda i: (0, 0)),
                  tab_spec, tab_spec, tab_spec,
                  pl.BlockSpec(memory_space=pl.ANY), pl.BlockSpec(memory_space=pl.ANY)],
        out_specs=[row_spec, layer_spec, row_spec, layer_spec, row_spec],
        out_shape=[jax.ShapeDtypeStruct((m, hd), BF), jax.ShapeDtypeStruct(kf_all.shape, F32),
                   jax.ShapeDtypeStruct((m, hd), BF), jax.ShapeDtypeStruct(vf_all.shape, F32),
                   jax.ShapeDtypeStruct((m, hd), BF)],
        input_output_aliases={5: 1, 6: 3},
        compiler_params=_cparams(("parallel",)),
        name=name,
    )(x, w_all, *tabs, kf_all, vf_all)


def _out_norm_kernel(a_ref, w_ref, x_ref, g_ref, b_ref, o_ref, ob_ref, *, alpha, rb):
    w = w_ref[...]
    for r in range(a_ref.shape[0] // rb):
        rows = pl.ds(r * rb, rb)
        sub = jnp.dot(a_ref[rows, :].astype(BF), w, preferred_element_type=F32)
        y = _layer_norm(alpha * x_ref[rows, :] + sub, g_ref[...], b_ref[...])
        o_ref[rows, :] = y
        ob_ref[rows, :] = y.astype(BF)


def _out_norm(a, w, x, g, b, *, alpha, tm, name):
    m, ka = a.shape
    d = w.shape[1]
    kern = functools.partial(_out_norm_kernel, alpha=alpha, rb=min(tm, 256))
    return pl.pallas_call(
        kern,
        grid=(m // tm,),
        in_specs=[pl.BlockSpec((tm, ka), lambda i: (i, 0)),
                  pl.BlockSpec((ka, d), lambda i: (0, 0)),
                  pl.BlockSpec((tm, d), lambda i: (i, 0)),
                  pl.BlockSpec((1, d), lambda i: (0, 0)),
                  pl.BlockSpec((1, d), lambda i: (0, 0))],
        out_specs=[pl.BlockSpec((tm, d), lambda i: (i, 0)),
                   pl.BlockSpec((tm, d), lambda i: (i, 0))],
        out_shape=[jax.ShapeDtypeStruct((m, d), F32), jax.ShapeDtypeStruct((m, d), BF)],
        compiler_params=_cparams(("parallel",)),
        name=name,
    )(a, w, x, g, b)


def _ffn_kernel(x_ref, xb_ref, wg_ref, wu_ref, wd_ref, g_ref, b_ref, o_ref, ob_ref, *, alpha, th):
    xb = xb_ref[...]
    hidden = wd_ref.shape[0]
    hs = []
    for c in range(hidden // th):
        cols = slice(c * th, (c + 1) * th)
        gate = jnp.dot(xb, wg_ref[:, cols], preferred_element_type=F32)
        up = jnp.dot(xb, wu_ref[:, cols], preferred_element_type=F32)
        hs.append((gate * _sigmoid(gate) * up).astype(BF))
    h = jnp.concatenate(hs, axis=1)
    sub = jnp.dot(h, wd_ref[...], preferred_element_type=F32)
    y = _layer_norm(alpha * x_ref[...] + sub, g_ref[...], b_ref[...])
    o_ref[...] = y
    ob_ref[...] = y.astype(BF)


def _ffn(x, xb, w_in, w_down, g, b, *, alpha, tm, th, name):
    m, d = x.shape
    hidden = w_down.shape[0]
    kern = functools.partial(_ffn_kernel, alpha=alpha, th=th)
    return pl.pallas_call(
        kern,
        grid=(m // tm,),
        in_specs=[pl.BlockSpec((tm, d), lambda i: (i, 0)),
                  pl.BlockSpec((tm, d), lambda i: (i, 0)),
                  _resident((d, hidden), lambda i: (0, 0)),
                  _resident((d, hidden), lambda i: (0, 1)),
                  _resident((hidden, d), lambda i: (0, 0)),
                  pl.BlockSpec((1, d), lambda i: (0, 0)),
                  pl.BlockSpec((1, d), lambda i: (0, 0))],
        out_specs=[pl.BlockSpec((tm, d), lambda i: (i, 0)),
                   pl.BlockSpec((tm, d), lambda i: (i, 0))],
        out_shape=[jax.ShapeDtypeStruct((m, d), F32), jax.ShapeDtypeStruct((m, d), BF)],
        compiler_params=_cparams(("parallel",)),
        name=name,
    )(x, xb, w_in, w_in, w_down, g, b)


def _ret_chunk_kernel(lg_ref, q_ref, k_ref, v_ref, g_ref, gn_ref, o_ref, st_ref, s_ref, decay_ref, *, chunk):
    c = pl.program_id(2)
    half = RET_DK // 2
    lg_row = lg_ref[0]
    lg = lg_row[:, :1]

    @pl.when(c == 0)
    def _():
        s_ref[...] = jnp.zeros_like(s_ref)
        ii = lax.broadcasted_iota(jnp.int32, (chunk, chunk), 0)
        jj = lax.broadcasted_iota(jnp.int32, (chunk, chunk), 1)
        rel = (ii - jj).astype(F32)
        decay_ref[...] = jnp.where(rel >= 0, jnp.exp(lg_row * jnp.maximum(rel, 0.0)), 0.0)

    idx = lax.broadcasted_iota(jnp.int32, (chunk, 1), 0).astype(F32)
    q_decay = jnp.exp(lg * (idx + 1.0))
    k_decay = jnp.exp(lg * (chunk - 1.0 - idx))
    chunk_decay = jnp.exp(lg * float(chunk))

    q = q_ref[...]
    k = k_ref[...]
    v = v_ref[...]
    s_old = s_ref[...]
    scores = lax.dot_general(q, k, (((1,), (1,)), ((), ())), preferred_element_type=F32) * decay_ref[...]
    inner = jnp.dot(scores.astype(BF), v, preferred_element_type=F32)
    cross = jnp.dot(q, s_old.astype(BF), preferred_element_type=F32) * q_decay
    kd_t = (k.astype(F32) * k_decay).T.astype(BF)
    s_ref[...] = chunk_decay * s_old + jnp.dot(kd_t, v, preferred_element_type=F32)

    o = inner + cross
    mu = jnp.mean(o, axis=-1, keepdims=True)
    d = o - mu
    var = jnp.mean(d * d, axis=-1, keepdims=True)
    on = d * lax.rsqrt(var + LN_EPS) * gn_ref[...]
    gt = g_ref[...].astype(F32)
    o_ref[...] = (gt * _sigmoid(gt) * on).astype(o_ref.dtype)

    @pl.when(c == pl.num_programs(2) - 1)
    def _():
        st_ref[0, 0, :, :RET_DV] = s_ref[:half, :]
        st_ref[0, 0, :, RET_DV:] = s_ref[half:, :]


def _ret_chunk(lg_tab, qk, v, g, gn_g, *, batch, heads, chunk, name):
    m = qk.shape[0]
    t = m // batch
    nc = t // chunk
    kern = functools.partial(_ret_chunk_kernel, chunk=chunk)
    return pl.pallas_call(
        kern,
        grid=(batch, heads, nc),
        in_specs=[pl.BlockSpec((1, 1, chunk), lambda b, h, c: (h, 0, 0)),
                  pl.BlockSpec((chunk, RET_DK), lambda b, h, c: (b * nc + c, h)),
                  pl.BlockSpec((chunk, RET_DK), lambda b, h, c: (b * nc + c, heads + h)),
                  pl.BlockSpec((chunk, RET_DV), lambda b, h, c: (b * nc + c, h)),
                  pl.BlockSpec((chunk, RET_DV), lambda b, h, c: (b * nc + c, h)),
                  pl.BlockSpec((1, RET_DV), lambda b, h, c: (0, h))],
        out_specs=[pl.BlockSpec((chunk, RET_DV), lambda b, h, c: (b * nc + c, h)),
                   pl.BlockSpec((1, 1, RET_DK // 2, 2 * RET_DV), lambda b, h, c: (b, h, 0, 0))],
        out_shape=[jax.ShapeDtypeStruct((m, heads * RET_DV), BF),
                   jax.ShapeDtypeStruct((batch, heads, RET_DK // 2, 2 * RET_DV), F32)],
        scratch_shapes=[pltpu.VMEM((RET_DK, RET_DV), F32), pltpu.VMEM((chunk, chunk), F32)],
        compiler_params=_cparams(("parallel", "parallel", "arbitrary")),
        name=name,
    )(lg_tab, qk, qk, v, g, gn_g)


def _ret_step_kernel(lg_ref, q_ref, k_ref, v_ref, g_ref, gn_ref, s0_ref, o_ref, *, heads):
    qrow = q_ref[0]
    krow = k_ref[0]
    vrow = v_ref[0]
    grow = g_ref[0]
    for h in range(heads):
        gamma = jnp.exp(lg_ref[h][:, :1])
        qh = qrow[:, h * RET_DK:(h + 1) * RET_DK]
        kh = krow[:, h * RET_DK:(h + 1) * RET_DK]
        vh = vrow[:, h * RET_DV:(h + 1) * RET_DV]
        s0 = s0_ref[0, 0, h]
        q8 = jnp.broadcast_to(qh, (8, RET_DK)).astype(BF)
        cross = jnp.dot(q8, s0.astype(BF), preferred_element_type=F32)[:1]
        qk = jnp.sum(qh * kh, axis=-1, keepdims=True)
        o = qk * vh + gamma * cross
        mu = jnp.mean(o, axis=-1, keepdims=True)
        d = o - mu
        var = jnp.mean(d * d, axis=-1, keepdims=True)
        on = d * lax.rsqrt(var + LN_EPS) * gn_ref[:, h * RET_DV:(h + 1) * RET_DV]
        gt = grow[:, h * RET_DV:(h + 1) * RET_DV]
        o_ref[0, :, h * RET_DV:(h + 1) * RET_DV] = (gt * _sigmoid(gt) * on).astype(o_ref.dtype)


def _ret_step(lg_tab, qk, v, g, gn_g, state_all, layer, *, heads, name):
    bs = qk.shape[0]
    hk = heads * RET_DK
    hv = heads * RET_DV
    q3 = qk.reshape(bs, 1, 2 * hk)
    v3 = v.reshape(bs, 1, hv)
    g3 = g.reshape(bs, 1, hv)
    kern = functools.partial(_ret_step_kernel, heads=heads)
    out = pl.pallas_call(
        kern,
        grid=(bs,),
        in_specs=[pl.BlockSpec(lg_tab.shape, lambda b: (0, 0, 0)),
                  pl.BlockSpec((1, 1, hk), lambda b: (b, 0, 0)),
                  pl.BlockSpec((1, 1, hk), lambda b: (b, 0, 1)),
                  pl.BlockSpec((1, 1, hv), lambda b: (b, 0, 0)),
                  pl.BlockSpec((1, 1, hv), lambda b: (b, 0, 0)),
                  pl.BlockSpec((1, hv), lambda b: (0, 0)),
                  pl.BlockSpec((1, 1, heads, RET_DK, RET_DV), lambda b: (layer, b, 0, 0, 0))],
        out_specs=pl.BlockSpec((1, 1, hv), lambda b: (b, 0, 0)),
        out_shape=jax.ShapeDtypeStruct((bs, 1, hv), BF),
        compiler_params=_cparams(("parallel",)),
        name=name,
    )(lg_tab, q3, q3, v3, g3, gn_g, state_all)
    return out.reshape(bs, hv)


def _ret_new_state_kernel(lg_ref, k_ref, v_ref, s0_ref, st_ref, *, heads):
    krow = k_ref[0, 0]
    vrow = v_ref[0, 0]
    for h in range(heads):
        gamma = jnp.exp(lg_ref[h][:, :1])
        kh = krow[:, h * RET_DK:(h + 1) * RET_DK]
        vh = vrow[:, h * RET_DV:(h + 1) * RET_DV]
        k_col = jnp.broadcast_to(kh, (8, RET_DK)).T[:, :1]
        st_ref[0, 0, h] = gamma * s0_ref[0, 0, h] + k_col * vh


def _ret_new_state(lg_tab, k_all, v_all, state_all, *, heads, name):
    n_layers, bs = state_all.shape[:2]
    hk = heads * RET_DK
    hv = heads * RET_DV
    kern = functools.partial(_ret_new_state_kernel, heads=heads)
    st_spec = pl.BlockSpec((1, 1, heads, RET_DK, RET_DV), lambda l, b: (l, b, 0, 0, 0))
    return pl.pallas_call(
        kern,
        grid=(n_layers, bs),
        in_specs=[pl.BlockSpec(lg_tab.shape, lambda l, b: (0, 0, 0)),
                  pl.BlockSpec((1, 1, 1, hk), lambda l, b: (l, b, 0, 0)),
                  pl.BlockSpec((1, 1, 1, hv), lambda l, b: (l, b, 0, 0)),
                  st_spec],
        out_specs=st_spec,
        out_shape=jax.ShapeDtypeStruct(state_all.shape, F32),
        compiler_params=_cparams(("parallel", "parallel")),
        name=name,
    )(lg_tab, k_all, v_all, state_all)


def _diff_lambda(lq1_ref, lk1_ref, lq2_ref, lk2_ref, lam_init):
    a = jnp.sum(lq1_ref[...] * lk1_ref[...], axis=-1, keepdims=True)
    b = jnp.sum(lq2_ref[...] * lk2_ref[...], axis=-1, keepdims=True)
    return jnp.exp(a) - jnp.exp(b) + lam_init


def _diff_flash_kernel(qi_ref, kj_ref, q_ref, k_ref, v_ref, lq1_ref, lk1_ref, lq2_ref, lk2_ref, sg_ref,
                       o_ref, q1_ref, q2_ref, m1_ref, m2_ref, l1_ref, l2_ref, a1_ref, a2_ref,
                       *, tq, tk, rb_full, rb_diag, lam_init):
    step = pl.program_id(2)
    qi = qi_ref[step]
    kj = kj_ref[step]
    last_kj = ((qi + 1) * tq - 1) // tk
    comps = ((q1_ref, m1_ref, l1_ref, a1_ref), (q2_ref, m2_ref, l2_ref, a2_ref))

    @pl.when(kj == 0)
    def _():
        q = q_ref[...]
        lane = lax.broadcasted_iota(jnp.int32, q.shape, 1)
        zero = jnp.zeros_like(q)
        q1_ref[...] = jnp.where(lane < DIFF_D, q, zero)
        q2_ref[...] = jnp.where(lane >= DIFF_D, q, zero)
        for _, m_ref, l_ref, a_ref in comps:
            m_ref[...] = jnp.full_like(m_ref, -jnp.inf)
            l_ref[...] = jnp.zeros_like(l_ref)
            a_ref[...] = jnp.zeros_like(a_ref)

    def update(masked):
        rb = rb_diag if masked else rb_full
        k = k_ref[...]
        v = v_ref[...]
        v_ext = jnp.concatenate([v, jnp.ones_like(v)], axis=1)
        tri = (lax.broadcasted_iota(jnp.int32, (rb, rb), 1) <= lax.broadcasted_iota(jnp.int32, (rb, rb), 0))
        def chain(qc_ref, m_ref, l_ref, a_ref, r):
            rows = pl.ds(r * rb, rb)
            q_r = qc_ref[rows, :]
            nt = (((1,), (1,)), ((), ()))
            if masked and tq == tk:
                lo = r * rb
                parts = []
                if lo > 0:
                    parts.append((lax.dot_general(q_r, k[:lo], nt, preferred_element_type=F32), v_ext[:lo]))
                s_d = lax.dot_general(q_r, k[lo:lo + rb], nt, preferred_element_type=F32)
                parts.append((jnp.where(tri, s_d, -jnp.inf), v_ext[lo:lo + rb]))
            else:
                s = lax.dot_general(q_r, k, nt, preferred_element_type=F32)
                if masked:
                    row = lax.broadcasted_iota(jnp.int32, (rb, tk), 0) + (qi * tq + r * rb)
                    col = lax.broadcasted_iota(jnp.int32, (rb, tk), 1) + kj * tk
                    s = jnp.where(col <= row, s, -jnp.inf)
                parts = [(s, v_ext)]
            m_prev = m_ref[rows, :]
            m_new = m_prev
            for s, _ in parts:
                m_new = jnp.maximum(m_new, jnp.max(s, axis=-1, keepdims=True))
            alpha = jnp.exp2(m_prev - m_new)
            pv = None
            for s, vx in parts:
                p = jnp.exp2(s - jnp.tile(m_new, (1, s.shape[1] // LANES)))
                d = jnp.dot(p.astype(BF), vx, preferred_element_type=F32)
                pv = d if pv is None else pv + d
            a_ref[rows, :] = alpha * a_ref[rows, :] + pv[:, :LANES]
            l_ref[rows, :] = alpha * l_ref[rows, :] + pv[:, LANES:]
            m_ref[rows, :] = m_new

        for qc_ref, m_ref, l_ref, a_ref in comps:
            for r in range(tq // rb):
                chain(qc_ref, m_ref, l_ref, a_ref, r)


    straddles = (kj + 1) * tk - 1 > qi * tq

    @pl.when(straddles)
    def _():
        update(True)

    @pl.when(jnp.logical_not(straddles))
    def _():
        update(False)

    @pl.when(kj == last_kj)
    def _():
        lam = _diff_lambda(lq1_ref, lk1_ref, lq2_ref, lk2_ref, lam_init)
        o = a1_ref[...] / l1_ref[...] - lam * (a2_ref[...] / l2_ref[...])
        o = o * lax.rsqrt(jnp.mean(o * o, axis=-1, keepdims=True) + LN_EPS)
        o_ref[...] = (o * sg_ref[...] * (1.0 - lam_init)).astype(o_ref.dtype)


def _diff_flash(q, k, v, lq1, lk1, lq2, lk2, subln_g, *, batch, heads, tq, tk, rb_full, rb_diag, lam_init, name):
    m = q.shape[0]
    t = m // batch
    nq, nk = t // tq, t // tk
    qi_list, kj_list = [], []
    for i in range(nq):
        for j in range(((i + 1) * tq - 1) // tk + 1):
            qi_list.append(i)
            kj_list.append(j)
    qi_tab = jnp.asarray(np.array(qi_list, np.int32))
    kj_tab = jnp.asarray(np.array(kj_list, np.int32))
    hd = 2 * DIFF_D
    assert hd == LANES
    kern = functools.partial(_diff_flash_kernel, tq=tq, tk=tk, rb_full=rb_full, rb_diag=rb_diag, lam_init=lam_init)
    vec_spec = pl.BlockSpec((1, DIFF_D), lambda b, h, s, qi, kj: (0, 0))
    grid_spec = pltpu.PrefetchScalarGridSpec(
        num_scalar_prefetch=2,
        grid=(batch, heads, len(qi_list)),
        in_specs=[pl.BlockSpec((tq, hd), lambda b, h, s, qi, kj: (b * nq + qi[s], h)),
                  pl.BlockSpec((tk, hd), lambda b, h, s, qi, kj: (b * nk + kj[s], h)),
                  pl.BlockSpec((tk, hd), lambda b, h, s, qi, kj: (b * nk + kj[s], h)),
                  vec_spec, vec_spec, vec_spec, vec_spec,
                  pl.BlockSpec((1, hd), lambda b, h, s, qi, kj: (0, h))],
        out_specs=pl.BlockSpec((tq, hd), lambda b, h, s, qi, kj: (b * nq + qi[s], h)),
        scratch_shapes=[pltpu.VMEM((tq, hd), BF), pltpu.VMEM((tq, hd), BF)]
                       + [pltpu.VMEM((tq, hd), F32) for _ in range(6)],
    )
    return pl.pallas_call(
        kern,
        grid_spec=grid_spec,
        out_shape=jax.ShapeDtypeStruct((m, heads * hd), BF),
        compiler_params=_cparams(("parallel", "parallel", "arbitrary")),
        name=name,
    )(qi_tab, kj_tab, q, k, v, lq1, lk1, lq2, lk2, subln_g)


def _diff_decode_kernel(pt_ref, q_ref, kn_ref, vn_ref, lq1_ref, lk1_ref, lq2_ref, lk2_ref, sg_ref, *rest,
                        heads, n_fetch, lam_init):
    del pt_ref
    k_refs = rest[:n_fetch]
    v_refs = rest[n_fetch:2 * n_fetch]
    o_ref, m_ref, l_ref, acc_ref = rest[2 * n_fetch:]
    p_idx = pl.program_id(1)
    rows = 2 * heads

    @pl.when(p_idx == 0)
    def _():
        m_ref[...] = jnp.full_like(m_ref, -jnp.inf)
        l_ref[...] = jnp.zeros_like(l_ref)
        acc_ref[...] = jnp.zeros_like(acc_ref)

    q8 = q_ref[0]
    lane = lax.broadcasted_iota(jnp.int32, q8.shape, 1)
    q16 = jnp.concatenate([jnp.where(lane < DIFF_D, q8, 0.0),
                           jnp.where(lane >= DIFF_D, q8, 0.0)], axis=0)
    q16b = q16.astype(BF)

    ncol = k_refs[0].shape[1]
    r_head = lax.broadcasted_iota(jnp.int32, (rows, ncol), 0) % heads
    c_head = lax.broadcasted_iota(jnp.int32, (rows, ncol), 1) % heads
    same_head = r_head == c_head
    s_pages = []
    for k_ref in k_refs:
        s = lax.dot_general(q16b, k_ref[0].astype(BF), (((1,), (1,)), ((), ())),
                            preferred_element_type=F32)
        s_pages.append(jnp.where(same_head, s, -jnp.inf))
    m_prev = m_ref[...]
    m_new = m_prev
    for s in s_pages:
        m_new = jnp.maximum(m_new, jnp.max(s, axis=-1, keepdims=True))
    alpha = jnp.exp(m_prev - m_new)
    l_new = alpha * l_ref[...]
    acc = alpha * acc_ref[...]
    for s, v_ref in zip(s_pages, v_refs):
        p = jnp.exp(s - m_new)
        l_new = l_new + jnp.sum(p, axis=-1, keepdims=True)
        acc = acc + jnp.dot(p.astype(BF), v_ref[0].astype(BF), preferred_element_type=F32)
    l_ref[...] = l_new
    acc_ref[...] = acc
    m_ref[...] = m_new

    @pl.when(p_idx == pl.num_programs(1) - 1)
    def _():
        kn = kn_ref[0]
        vn = vn_ref[0]
        kn2 = jnp.concatenate([kn, kn], axis=0)
        vn2 = jnp.concatenate([vn, vn], axis=0)
        s_n = jnp.sum(q16 * kn2, axis=-1, keepdims=True)
        m_fin = jnp.maximum(m_new, s_n)
        beta = jnp.exp(m_new - m_fin)
        p_n = jnp.exp(s_n - m_fin)
        l_fin = beta * l_new + p_n
        acc_fin = beta * acc + p_n * vn2
        lam = _diff_lambda(lq1_ref, lk1_ref, lq2_ref, lk2_ref, lam_init)
        o = acc_fin[:heads] / l_fin[:heads] - lam * (acc_fin[heads:] / l_fin[heads:])
        o = o * lax.rsqrt(jnp.mean(o * o, axis=-1, keepdims=True) + LN_EPS)
        o_ref[0] = (o * sg_ref[...] * (1.0 - lam_init)).astype(o_ref.dtype)


def _diff_decode(page_table, q, kn, vn, lq1, lk1, lq2, lk2, subln_g, cache_k, cache_v, layer, *,
                 heads, lam_init, name):
    bs, n_pages = page_table.shape
    hd = 2 * DIFF_D
    n_layers, n_phys = cache_k.shape[:2]
    ck = cache_k.reshape(n_layers * n_phys, PAGE_SIZE * heads, hd)
    cv = cache_v.reshape(n_layers * n_phys, PAGE_SIZE * heads, hd)
    q3 = q.reshape(bs, heads, hd)
    kn3 = kn.reshape(bs, heads, hd)
    vn3 = vn.reshape(bs, heads, hd)
    sg = subln_g.reshape(heads, hd)
    pt = page_table.reshape(-1)
    n_fetch = DECODE_PAGES if n_pages % DECODE_PAGES == 0 else 1
    base = layer * n_phys
    kern = functools.partial(_diff_decode_kernel, heads=heads, n_fetch=n_fetch, lam_init=lam_init)
    row_spec = pl.BlockSpec((1, heads, hd), lambda b, p, pt: (b, 0, 0))
    vec_spec = pl.BlockSpec((1, DIFF_D), lambda b, p, pt: (0, 0))

    def page_spec(r):
        return pl.BlockSpec((1, PAGE_SIZE * heads, hd),
                            lambda b, p, pt: (base + pt[b * n_pages + p * n_fetch + r], 0, 0))

    page_specs = [page_spec(r) for r in range(n_fetch)]
    grid_spec = pltpu.PrefetchScalarGridSpec(
        num_scalar_prefetch=1,
        grid=(bs, n_pages // n_fetch),
        in_specs=[row_spec, row_spec, row_spec, vec_spec, vec_spec, vec_spec, vec_spec,
                  pl.BlockSpec((heads, hd), lambda b, p, pt: (0, 0))] + page_specs + page_specs,
        out_specs=pl.BlockSpec((1, heads, hd), lambda b, p, pt: (b, 0, 0)),
        scratch_shapes=[pltpu.VMEM((2 * heads, 1), F32),
                        pltpu.VMEM((2 * heads, 1), F32),
                        pltpu.VMEM((2 * heads, hd), F32)],
    )
    out = pl.pallas_call(
        kern,
        grid_spec=grid_spec,
        out_shape=jax.ShapeDtypeStruct((bs, heads, hd), BF),
        compiler_params=_cparams(("parallel", "arbitrary")),
        name=name,
    )(pt, q3, kn3, vn3, lq1, lk1, lq2, lk2, sg, *([ck] * n_fetch), *([cv] * n_fetch))
    return out.reshape(bs, heads * hd)


def _ret_tables(pos):
    inv = 1.0 / (RET_ROPE_BASE ** jnp.linspace(0.0, 1.0, RET_DK // 2, dtype=F32))
    ang = pos.astype(F32)[:, None] * inv[None, :]
    return jnp.cos(ang), jnp.sin(ang)


def _ret_tables_interleaved(pos, heads):
    cos, sin = _ret_tables(pos)
    zero = jnp.zeros_like(sin)
    c = jnp.stack([cos, cos], axis=-1).reshape(pos.shape[0], RET_DK)
    a = jnp.stack([-sin, zero], axis=-1).reshape(pos.shape[0], RET_DK)
    b = jnp.stack([zero, sin], axis=-1).reshape(pos.shape[0], RET_DK)
    k_scale = RET_DK ** -0.5
    return tuple(jnp.concatenate([jnp.tile(tab, (1, heads)), jnp.tile(tab * k_scale, (1, heads))], axis=1)
                 for tab in (c, a, b))


def _diff_tables(pos):
    half = DIFF_ROT // 2
    inv = 1.0 / (ROPE_THETA ** (jnp.arange(half, dtype=F32) * 2.0 / DIFF_ROT))
    ang = pos.astype(F32)[:, None] * inv[None, :]
    cos, sin = jnp.cos(ang), jnp.sin(ang)
    n = pos.shape[0]
    ones = jnp.ones((n, DIFF_D - DIFF_ROT), F32)
    zeros_rest = jnp.zeros((n, DIFF_D - DIFF_ROT), F32)
    zeros_h = jnp.zeros((n, half), F32)
    c = jnp.concatenate([cos, cos, ones], axis=1)
    a = jnp.concatenate([-sin, zeros_h, zeros_rest], axis=1)
    b = jnp.concatenate([zeros_h, sin, zeros_rest], axis=1)
    rep = LANES // DIFF_D
    return tuple(jnp.tile(tab, (1, rep)) for tab in (c, a, b))


def _ret_split_weights(w_in, heads):
    hk, hv = heads * RET_DK, heads * RET_DV
    perm = np.concatenate([np.arange(0, RET_DK, 2), np.arange(1, RET_DK, 2)])
    cols = np.concatenate([h * RET_DK + perm for h in range(heads)])
    w_q = w_in[:, :hk][:, cols]
    w_k = w_in[:, hk:2 * hk][:, cols]
    w_qk = jnp.concatenate([w_q, w_k], axis=1).astype(BF)
    w_v = w_in[:, 2 * hk:2 * hk + hv].astype(BF)
    w_g = w_in[:, 2 * hk + hv:].astype(BF)
    return w_qk, w_v, w_g


def kernel(x_prompt, x_sample, state_ret, cache_k, cache_v, page_table, ret_w_in, ret_gn_g, ret_w_o, diff_w_in, diff_lq1, diff_lk1, diff_lq2, diff_lk2, diff_subln_g, diff_w_o, ffn_w_in, ffn_w_down, ln1_g, ln1_b, ln2_g, ln2_b):
    bp, tp, d_model = x_prompt.shape
    bs, ts, _ = x_sample.shape
    assert ts == 1
    depth = ffn_w_in.shape[0]
    ret_heads = ret_w_o.shape[1] // RET_DV
    diff_heads = diff_w_o.shape[1] // (2 * DIFF_D)
    hd = diff_heads * 2 * DIFF_D
    past_len = page_table.shape[1] * PAGE_SIZE
    alpha = (2.0 * depth) ** 0.25
    mp = bp * tp

    pos_p = jnp.arange(tp)
    pos_s = jnp.broadcast_to(past_len + jnp.arange(ts), (bs,))
    ret_tab_p = _ret_tables(pos_p)
    ret_tab_s = _ret_tables_interleaved(pos_s, ret_heads)
    diff_tab_p = _diff_tables(pos_p)
    diff_tab_s = _diff_tables(pos_s)

    ret_chunk = RET_CHUNK if tp % RET_CHUNK == 0 else tp
    lg = jnp.log(1.0 - 2.0 ** (-5.0 - jnp.arange(ret_heads, dtype=F32)))
    lg_tab = jnp.broadcast_to(lg[:, None, None], (ret_heads, 1, max(ret_chunk, LANES)))

    tm_p = _row_tile(mp, 512)
    tm_o = _row_tile(mp, 512)
    tm_s = bs

    xp = x_prompt.reshape(mp, d_model)
    xs = x_sample.reshape(bs * ts, d_model)
    xp_b, xs_b = xp, xs

    n_diff = diff_w_in.shape[0]
    ret_p, kc_s, vc_s, ret_k_s, ret_v_s = [], [], [], [], []
    kc_p_all = jnp.zeros((n_diff, mp, hd), F32)
    vc_p_all = jnp.zeros((n_diff, mp, hd), F32)
    for i in range(depth):
        j = i // N_MIXERS
        g1, b1 = ln1_g[i][None, :], ln1_b[i][None, :]
        g2, b2 = ln2_g[i][None, :], ln2_b[i][None, :]
        if i % N_MIXERS == 0:
            w_qk, w_v, w_g = _ret_split_weights(ret_w_in[j], ret_heads)
            w_o = ret_w_o[j].astype(BF)
            gn = ret_gn_g[j][None, :]
            w_all = jnp.concatenate([w_qk, w_v, w_g], axis=1)
            qk, v, g = _ret_proj(xp_b, w_all, *ret_tab_p, heads=ret_heads, tm=tm_p, name=f"ret{j}_proj_p")
            mix_p, st_p = _ret_chunk(lg_tab, qk, v, g, gn, batch=bp, heads=ret_heads, chunk=ret_chunk,
                                     name=f"ret{j}_chunk_p")
            ret_p.append(st_p.reshape(bp, ret_heads, RET_DK, RET_DV))
            w_qk_nat = ret_w_in[j][:, :2 * ret_heads * RET_DK].astype(BF)
            (qk_s,) = _shift_rope_proj(xs_b, w_qk_nat, ret_tab_s, [F32], scale=1.0, shift=1, tm=tm_s, tn=512,
                                       name=f"ret{j}_qk_s")
            (v_s,) = _proj_plain(xs_b, w_v, [F32], tm=tm_s, tn=512, name=f"ret{j}_v_s")
            (g_s,) = _proj_plain(xs_b, w_g, [F32], tm=tm_s, tn=512, name=f"ret{j}_g_s")
            mix_s = _ret_step(lg_tab, qk_s, v_s, g_s, gn, state_ret, j, heads=ret_heads, name=f"ret{j}_step_s")
            ret_k_s.append(qk_s[:, ret_heads * RET_DK:])
            ret_v_s.append(v_s)
        else:
            lam_init = 0.8 - 0.6 * math.exp(-0.3 * (i + 1))
            w_in = diff_w_in[j].astype(BF)
            w_q, w_k, w_v = w_in[:, :hd], w_in[:, hd:2 * hd], w_in[:, 2 * hd:]
            w_o = diff_w_o[j].astype(BF)
            lam_args = (diff_lq1[j][None, :], diff_lk1[j][None, :], diff_lq2[j][None, :], diff_lk2[j][None, :])
            sg = diff_subln_g[j][None, :]
            rot = DIFF_ROT // 2
            q, kc_p_all, k_b, vc_p_all, v_b = _diff_proj(xp_b, w_in, diff_tab_p, kc_p_all, vc_p_all, j,
                                                         q_scale=DIFF_D ** -0.5 * math.log2(math.e), shift=rot,
                                                         tm=tm_p, name=f"diff{j}_proj_p")
            tq = FLASH_TQ if tp % FLASH_TQ == 0 else tp
            tk = FLASH_TK if tp % FLASH_TK == 0 else tp
            mix_p = _diff_flash(q, k_b, v_b, *lam_args, sg, batch=bp, heads=diff_heads, tq=tq, tk=tk,
                                rb_full=min(FLASH_ROWS, tq), rb_diag=min(FLASH_ROWS_DIAG, tq),
                                lam_init=lam_init, name=f"diff{j}_flash_p")
            (q_s,) = _shift_rope_proj(xs_b, w_q, diff_tab_s, [F32], scale=DIFF_D ** -0.5, shift=rot, tm=tm_s, tn=512,
                                      name=f"diff{j}_q_s")
            (kn,) = _shift_rope_proj(xs_b, w_k, diff_tab_s, [F32], scale=1.0, shift=rot, tm=tm_s, tn=512,
                                     name=f"diff{j}_k_s")
            (vn,) = _proj_plain(xs_b, w_v, [F32], tm=tm_s, tn=512, name=f"diff{j}_v_s")
            mix_s = _diff_decode(page_table, q_s, kn, vn, *lam_args, sg, cache_k, cache_v, j,
                                 heads=diff_heads, lam_init=lam_init, name=f"diff{j}_decode_s")
            kc_s.append(kn.reshape(bs, ts, diff_heads, 2 * DIFF_D))
            vc_s.append(vn.reshape(bs, ts, diff_heads, 2 * DIFF_D))

        xp, xp_b = _out_norm(mix_p, w_o, xp, g1, b1, alpha=alpha, tm=tm_o, name=f"l{i}_mix_norm_p")
        xs, xs_b = _out_norm(mix_s, w_o, xs, g1, b1, alpha=alpha, tm=tm_s, name=f"l{i}_mix_norm_s")
        w_fi = ffn_w_in[i].astype(BF)
        w_fd = ffn_w_down[i].astype(BF)
        xp, xp_b = _ffn(xp, xp_b, w_fi, w_fd, g2, b2, alpha=alpha, tm=_row_tile(mp, FFN_ROWS), th=256,
                        name=f"l{i}_ffn_p")
        xs, xs_b = _ffn(xs, xs_b, w_fi, w_fd, g2, b2, alpha=alpha, tm=tm_s, th=256, name=f"l{i}_ffn_s")

    ret_s_all = _ret_new_state(lg_tab, jnp.stack(ret_k_s)[:, :, None, :], jnp.stack(ret_v_s)[:, :, None, :],
                               state_ret, heads=ret_heads, name="ret_new_state_s")
    return (xp.reshape(bp, tp, d_model), xs.reshape(bs, ts, d_model),
            jnp.stack(ret_p),
            kc_p_all.reshape(n_diff, bp, tp, diff_heads, 2 * DIFF_D),
            vc_p_all.reshape(n_diff, bp, tp, diff_heads, 2 * DIFF_D),
            ret_s_all, jnp.stack(kc_s), jnp.stack(vc_s))
```

```python
import functools
import math

import numpy as np
import jax
import jax.numpy as jnp
from jax import lax
from jax.experimental import pallas as pl
from jax.experimental.pallas import tpu as pltpu

BF = jnp.bfloat16
F32 = jnp.float32

RET_DK = 256
RET_DV = 512
RET_ROPE_BASE = 10000.0
DIFF_D = 64
DIFF_ROT = DIFF_D // 4
ROPE_THETA = 500000.0
PAGE_SIZE = 128
LN_EPS = 1e-5
N_MIXERS = 2

LANES = 128
SUBLANES = 8
PROJ_ROWS = 512
PROJ_COLS = 512
FFN_COLS = 256
NORM_ROWS = 256
FLASH_TQ = 2048
FLASH_TK = 2048
FLASH_ROWS = 256
FLASH_ROWS_DIAG = 512
RET_CHUNK = 512
FFN_ROWS = 512
DECODE_PAGES = 8
VMEM_LIMIT = 48 * 1024 * 1024


def _cparams(sem):
    return pltpu.CompilerParams(dimension_semantics=sem, vmem_limit_bytes=VMEM_LIMIT)


def _sigmoid(x):
    return 1.0 / (1.0 + jnp.exp(-x))


def _layer_norm(y, g, b):
    mu = jnp.mean(y, axis=-1, keepdims=True)
    d = y - mu
    var = jnp.mean(d * d, axis=-1, keepdims=True)
    return d * lax.rsqrt(var + LN_EPS) * g + b


def _row_tile(m, pref):
    return pref if m % pref == 0 else m


def _proj_plain_kernel(x_ref, w_ref, *o_refs):
    acc = jnp.dot(x_ref[...].astype(BF), w_ref[...], preferred_element_type=F32)
    for o_ref in o_refs:
        o_ref[...] = acc.astype(o_ref.dtype)


def _proj_plain(x, w, out_dtypes, *, tm, tn, name):
    m, k = x.shape
    n = w.shape[1]
    return pl.pallas_call(
        _proj_plain_kernel,
        grid=(m // tm, n // tn),
        in_specs=[pl.BlockSpec((tm, k), lambda i, j: (i, 0)),
                  pl.BlockSpec((k, tn), lambda i, j: (0, j))],
        out_specs=[pl.BlockSpec((tm, tn), lambda i, j: (i, j)) for _ in out_dtypes],
        out_shape=[jax.ShapeDtypeStruct((m, n), dt) for dt in out_dtypes],
        compiler_params=_cparams(("parallel", "parallel")),
        name=name,
    )(x, w)


def _shift_rope_kernel(x_ref, w_ref, c_ref, a_ref, b_ref, *o_refs, scale, shift):
    acc = jnp.dot(x_ref[...].astype(BF), w_ref[...], preferred_element_type=F32)
    tab_tiles = c_ref.shape[1] // LANES
    for t in range(acc.shape[1] // LANES):
        tt = t % tab_tiles
        cols = slice(tt * LANES, (tt + 1) * LANES)
        xs = acc[:, t * LANES:(t + 1) * LANES]
        up = pltpu.roll(xs, LANES - shift, axis=1)
        dn = pltpu.roll(xs, shift, axis=1)
        r = xs * c_ref[:, cols] + up * a_ref[:, cols] + dn * b_ref[:, cols]
        if scale != 1.0:
            r = r * scale
        for o_ref in o_refs:
            o_ref[:, t * LANES:(t + 1) * LANES] = r.astype(o_ref.dtype)


def _shift_rope_proj(x, w, tabs, out_dtypes, *, scale, shift, tm, tn, name):
    m, k = x.shape
    n = w.shape[1]
    ntab = tabs[0].shape[0] // tm
    tw = min(tabs[0].shape[1], tn)
    ntab_cols = tabs[0].shape[1] // tw
    kern = functools.partial(_shift_rope_kernel, scale=scale, shift=shift)
    tab_spec = pl.BlockSpec((tm, tw), lambda i, j: (i % ntab, j % ntab_cols))
    return pl.pallas_call(
        kern,
        grid=(m // tm, n // tn),
        in_specs=[pl.BlockSpec((tm, k), lambda i, j: (i, 0)),
                  pl.BlockSpec((k, tn), lambda i, j: (0, j)),
                  tab_spec, tab_spec, tab_spec],
        out_specs=[pl.BlockSpec((tm, tn), lambda i, j: (i, j)) for _ in out_dtypes],
        out_shape=[jax.ShapeDtypeStruct((m, n), dt) for dt in out_dtypes],
        compiler_params=_cparams(("parallel", "parallel")),
        name=name,
    )(x, w, *tabs)


def _resident(block_shape, index_map):
    return pl.BlockSpec(block_shape, index_map, pipeline_mode=pl.Buffered(1))


def _ret_proj_kernel(x_ref, w_ref, cos_ref, sin_ref, qk_ref, v_ref, g_ref, *, heads, tn):
    xb = x_ref[...].astype(BF)
    hk, hv = heads * RET_DK, heads * RET_DV
    half = RET_DK // 2
    c = cos_ref[...]
    s = sin_ref[...]
    for h in range(2 * heads):
        acc = jnp.dot(xb, w_ref[:, h * RET_DK:(h + 1) * RET_DK], preferred_element_type=F32)
        x1 = acc[:, :half]
        x2 = acc[:, half:]
        r1 = x1 * c - x2 * s
        r2 = x2 * c + x1 * s
        if h >= heads:
            r1 = r1 * RET_DK ** -0.5
            r2 = r2 * RET_DK ** -0.5
        qk_ref[:, h * RET_DK:h * RET_DK + half] = r1.astype(qk_ref.dtype)
        qk_ref[:, h * RET_DK + half:(h + 1) * RET_DK] = r2.astype(qk_ref.dtype)
    for t in range(hv // tn):
        cols = slice(t * tn, (t + 1) * tn)
        v_ref[:, cols] = jnp.dot(xb, w_ref[:, 2 * hk + t * tn:2 * hk + (t + 1) * tn],
                                 preferred_element_type=F32).astype(v_ref.dtype)
    for t in range(hv // tn):
        cols = slice(t * tn, (t + 1) * tn)
        g_ref[:, cols] = jnp.dot(xb, w_ref[:, 2 * hk + hv + t * tn:2 * hk + hv + (t + 1) * tn],
                                 preferred_element_type=F32).astype(g_ref.dtype)


def _ret_proj(x, w_all, cos, sin, *, heads, tm, name):
    m, k = x.shape
    hk, hv = heads * RET_DK, heads * RET_DV
    ntab = cos.shape[0] // tm
    kern = functools.partial(_ret_proj_kernel, heads=heads, tn=PROJ_COLS)
    tab_spec = pl.BlockSpec((tm, RET_DK // 2), lambda i: (i % ntab, 0))
    return pl.pallas_call(
        kern,
        grid=(m // tm,),
        in_specs=[pl.BlockSpec((tm, k), lambda i: (i, 0)),
                  _resident((k, 2 * hk + 2 * hv), lambda i: (0, 0)),
                  tab_spec, tab_spec],
        out_specs=[pl.BlockSpec((tm, 2 * hk), lambda i: (i, 0)),
                   pl.BlockSpec((tm, hv), lambda i: (i, 0)),
                   pl.BlockSpec((tm, hv), lambda i: (i, 0))],
        out_shape=[jax.ShapeDtypeStruct((m, 2 * hk), BF),
                   jax.ShapeDtypeStruct((m, hv), BF),
                   jax.ShapeDtypeStruct((m, hv), F32)],
        compiler_params=_cparams(("parallel",)),
        name=name,
    )(x, w_all, cos, sin)


def _diff_proj_kernel(*refs, hd, q_scale, shift, tn, layer, first):
    if first:
        x_ref, w_ref, c_ref, a_ref, b_ref, q_ref, kf_ref, kb_ref, vf_ref, vb_ref = refs
        for l in range(kf_ref.shape[0]):
            if l != layer:
                kf_ref[l] = jnp.zeros(kf_ref.shape[1:], F32)
                vf_ref[l] = jnp.zeros(vf_ref.shape[1:], F32)
        kf_dst, vf_dst = kf_ref.at[layer], vf_ref.at[layer]
    else:
        x_ref, w_ref, c_ref, a_ref, b_ref, _, _, q_ref, kf_dst, kb_ref, vf_dst, vb_ref = refs
    xb = x_ref[...].astype(BF)
    c = c_ref[...]
    a = a_ref[...]
    b = b_ref[...]

    def rope(acc, scale, col0, o_refs):
        for t in range(acc.shape[1] // LANES):
            xs = acc[:, t * LANES:(t + 1) * LANES]
            up = pltpu.roll(xs, LANES - shift, axis=1)
            dn = pltpu.roll(xs, shift, axis=1)
            r = xs * c + up * a + dn * b
            if scale != 1.0:
                r = r * scale
            for o_ref in o_refs:
                o_ref[:, col0 + t * LANES:col0 + (t + 1) * LANES] = r.astype(o_ref.dtype)

    for t in range(hd // tn):
        acc = jnp.dot(xb, w_ref[:, t * tn:(t + 1) * tn], preferred_element_type=F32)
        rope(acc, q_scale, t * tn, (q_ref,))
    for t in range(hd // tn):
        acc = jnp.dot(xb, w_ref[:, hd + t * tn:hd + (t + 1) * tn], preferred_element_type=F32)
        rope(acc, 1.0, t * tn, (kf_dst, kb_ref))
    for t in range(hd // tn):
        cols = slice(t * tn, (t + 1) * tn)
        acc = jnp.dot(xb, w_ref[:, 2 * hd + t * tn:2 * hd + (t + 1) * tn], preferred_element_type=F32)
        vf_dst[:, cols] = acc
        vb_ref[:, cols] = acc.astype(BF)


def _diff_proj(x, w_all, tabs, kf_all, vf_all, layer, n_layers, *, q_scale, shift, tm, name):
    m, k = x.shape
    hd = w_all.shape[1] // 3
    ntab = tabs[0].shape[0] // tm
    first = kf_all is None
    kern = functools.partial(_diff_proj_kernel, hd=hd, q_scale=q_scale, shift=shift, tn=PROJ_COLS, layer=layer,
                             first=first)
    tab_spec = pl.BlockSpec((tm, LANES), lambda i: (i % ntab, 0))
    row_spec = pl.BlockSpec((tm, hd), lambda i: (i, 0))
    in_specs = [pl.BlockSpec((tm, k), lambda i: (i, 0)),
                _resident((k, 3 * hd), lambda i: (0, 0)),
                tab_spec, tab_spec, tab_spec]
    args = [x, w_all, *tabs]
    if first:
        stack_spec = pl.BlockSpec((n_layers, tm, hd), lambda i: (0, i, 0))
        aliases = {}
    else:
        stack_spec = pl.BlockSpec((None, tm, hd), lambda i: (layer, i, 0))
        in_specs += [pl.BlockSpec(memory_space=pl.ANY), pl.BlockSpec(memory_space=pl.ANY)]
        args += [kf_all, vf_all]
        aliases = {5: 1, 6: 3}
    stacked = jax.ShapeDtypeStruct((n_layers, m, hd), F32)
    return pl.pallas_call(
        kern,
        grid=(m // tm,),
        in_specs=in_specs,
        out_specs=[row_spec, stack_spec, row_spec, stack_spec, row_spec],
        out_shape=[jax.ShapeDtypeStruct((m, hd), BF), stacked, jax.ShapeDtypeStruct((m, hd), BF), stacked,
                   jax.ShapeDtypeStruct((m, hd), BF)],
        input_output_aliases=aliases,
        compiler_params=_cparams(("parallel",)),
        name=name,
    )(*args)


def _mix_ffn_kernel(a_ref, wo_ref, x_ref, g1_ref, b1_ref, wg_ref, wu_ref, wd_ref, g2_ref, b2_ref,
                    o_ref, ob_ref, *, alpha, th, rb):
    wo = wo_ref[...]
    x1_rows = []
    for r in range(a_ref.shape[0] // rb):
        rows = pl.ds(r * rb, rb)
        sub = jnp.dot(a_ref[rows, :].astype(BF), wo, preferred_element_type=F32)
        x1_rows.append(_layer_norm(alpha * x_ref[rows, :] + sub, g1_ref[...], b1_ref[...]))
    x1 = jnp.concatenate(x1_rows, axis=0)
    xb = x1.astype(BF)
    hidden = wd_ref.shape[0]
    hs = []
    for c in range(hidden // th):
        cols = slice(c * th, (c + 1) * th)
        gate = jnp.dot(xb, wg_ref[:, cols], preferred_element_type=F32)
        up = jnp.dot(xb, wu_ref[:, cols], preferred_element_type=F32)
        hs.append((gate * _sigmoid(gate) * up).astype(BF))
    h = jnp.concatenate(hs, axis=1)
    sub = jnp.dot(h, wd_ref[...], preferred_element_type=F32)
    y = _layer_norm(alpha * x1 + sub, g2_ref[...], b2_ref[...])
    o_ref[...] = y
    ob_ref[...] = y.astype(BF)


def _mix_ffn(a, w_o, x, g1, b1, w_in, w_down, g2, b2, *, alpha, tm, th, name):
    m, ka = a.shape
    d = x.shape[1]
    hidden = w_down.shape[0]
    kern = functools.partial(_mix_ffn_kernel, alpha=alpha, th=th, rb=min(tm, NORM_ROWS))
    vec = pl.BlockSpec((1, d), lambda i: (0, 0))
    return pl.pallas_call(
        kern,
        grid=(m // tm,),
        in_specs=[pl.BlockSpec((tm, ka), lambda i: (i, 0)),
                  _resident((ka, d), lambda i: (0, 0)),
                  pl.BlockSpec((tm, d), lambda i: (i, 0)),
                  vec, vec,
                  _resident((d, hidden), lambda i: (0, 0)),
                  _resident((d, hidden), lambda i: (0, 1)),
                  _resident((hidden, d), lambda i: (0, 0)),
                  vec, vec],
        out_specs=[pl.BlockSpec((tm, d), lambda i: (i, 0)),
                   pl.BlockSpec((tm, d), lambda i: (i, 0))],
        out_shape=[jax.ShapeDtypeStruct((m, d), F32), jax.ShapeDtypeStruct((m, d), BF)],
        compiler_params=_cparams(("parallel",)),
        name=name,
    )(a, w_o, x, g1, b1, w_in, w_in, w_down, g2, b2)


def _ret_chunk_kernel(lg_ref, q_ref, k_ref, v_ref, g_ref, gn_ref, o_ref, st_ref, s_ref, decay_ref, *, chunk):
    c = pl.program_id(2)
    half = RET_DK // 2
    lg_row = lg_ref[0]
    lg = lg_row[:, :1]

    @pl.when(c == 0)
    def _():
        s_ref[...] = jnp.zeros_like(s_ref)
        ii = lax.broadcasted_iota(jnp.int32, (chunk, chunk), 0)
        jj = lax.broadcasted_iota(jnp.int32, (chunk, chunk), 1)
        rel = (ii - jj).astype(F32)
        decay_ref[...] = jnp.where(rel >= 0, jnp.exp(lg_row * jnp.maximum(rel, 0.0)), 0.0)

    idx = lax.broadcasted_iota(jnp.int32, (chunk, 1), 0).astype(F32)
    q_decay = jnp.exp(lg * (idx + 1.0))
    k_decay = jnp.exp(lg * (chunk - 1.0 - idx))
    chunk_decay = jnp.exp(lg * float(chunk))

    q = q_ref[...]
    k = k_ref[...]
    v = v_ref[...]
    s_old = s_ref[...]
    scores = lax.dot_general(q, k, (((1,), (1,)), ((), ())), preferred_element_type=F32) * decay_ref[...]
    inner = jnp.dot(scores.astype(BF), v, preferred_element_type=F32)
    cross = jnp.dot(q, s_old.astype(BF), preferred_element_type=F32) * q_decay
    kd_t = (k.astype(F32) * k_decay).T.astype(BF)
    s_ref[...] = chunk_decay * s_old + jnp.dot(kd_t, v, preferred_element_type=F32)

    o = inner + cross
    mu = jnp.mean(o, axis=-1, keepdims=True)
    d = o - mu
    var = jnp.mean(d * d, axis=-1, keepdims=True)
    on = d * lax.rsqrt(var + LN_EPS) * gn_ref[...]
    gt = g_ref[...].astype(F32)
    o_ref[...] = (gt * _sigmoid(gt) * on).astype(o_ref.dtype)

    @pl.when(c == pl.num_programs(2) - 1)
    def _():
        st_ref[0, 0, :, :RET_DV] = s_ref[:half, :]
        st_ref[0, 0, :, RET_DV:] = s_ref[half:, :]


def _ret_chunk(lg_tab, qk, v, g, gn_g, *, batch, heads, chunk, name):
    m = qk.shape[0]
    t = m // batch
    nc = t // chunk
    kern = functools.partial(_ret_chunk_kernel, chunk=chunk)
    return pl.pallas_call(
        kern,
        grid=(batch, heads, nc),
        in_specs=[pl.BlockSpec((1, 1, chunk), lambda b, h, c: (h, 0, 0)),
                  pl.BlockSpec((chunk, RET_DK), lambda b, h, c: (b * nc + c, h)),
                  pl.BlockSpec((chunk, RET_DK), lambda b, h, c: (b * nc + c, heads + h)),
                  pl.BlockSpec((chunk, RET_DV), lambda b, h, c: (b * nc + c, h)),
                  pl.BlockSpec((chunk, RET_DV), lambda b, h, c: (b * nc + c, h)),
                  pl.BlockSpec((1, RET_DV), lambda b, h, c: (0, h))],
        out_specs=[pl.BlockSpec((chunk, RET_DV), lambda b, h, c: (b * nc + c, h)),
                   pl.BlockSpec((1, 1, RET_DK // 2, 2 * RET_DV), lambda b, h, c: (b, h, 0, 0))],
        out_shape=[jax.ShapeDtypeStruct((m, heads * RET_DV), BF),
                   jax.ShapeDtypeStruct((batch, heads, RET_DK // 2, 2 * RET_DV), F32)],
        scratch_shapes=[pltpu.VMEM((RET_DK, RET_DV), F32), pltpu.VMEM((chunk, chunk), F32)],
        compiler_params=_cparams(("parallel", "parallel", "arbitrary")),
        name=name,
    )(lg_tab, qk, qk, v, g, gn_g)


def _ret_step_kernel(lg_ref, q_ref, k_ref, v_ref, g_ref, gn_ref, s0_ref, o_ref, *, heads):
    qrow = q_ref[0]
    krow = k_ref[0]
    vrow = v_ref[0]
    grow = g_ref[0]
    for h in range(heads):
        gamma = jnp.exp(lg_ref[h][:, :1])
        qh = qrow[:, h * RET_DK:(h + 1) * RET_DK]
        kh = krow[:, h * RET_DK:(h + 1) * RET_DK]
        vh = vrow[:, h * RET_DV:(h + 1) * RET_DV]
        s0 = s0_ref[0, 0, h]
        q8 = jnp.broadcast_to(qh, (SUBLANES, RET_DK)).astype(BF)
        cross = jnp.dot(q8, s0.astype(BF), preferred_element_type=F32)[:1]
        qk = jnp.sum(qh * kh, axis=-1, keepdims=True)
        o = qk * vh + gamma * cross
        mu = jnp.mean(o, axis=-1, keepdims=True)
        d = o - mu
        var = jnp.mean(d * d, axis=-1, keepdims=True)
        on = d * lax.rsqrt(var + LN_EPS) * gn_ref[:, h * RET_DV:(h + 1) * RET_DV]
        gt = grow[:, h * RET_DV:(h + 1) * RET_DV]
        o_ref[0, :, h * RET_DV:(h + 1) * RET_DV] = (gt * _sigmoid(gt) * on).astype(o_ref.dtype)


def _ret_step(lg_tab, qk, v, g, gn_g, state_all, layer, *, heads, name):
    bs = qk.shape[0]
    hk = heads * RET_DK
    hv = heads * RET_DV
    q3 = qk.reshape(bs, 1, 2 * hk)
    v3 = v.reshape(bs, 1, hv)
    g3 = g.reshape(bs, 1, hv)
    kern = functools.partial(_ret_step_kernel, heads=heads)
    out = pl.pallas_call(
        kern,
        grid=(bs,),
        in_specs=[pl.BlockSpec(lg_tab.shape, lambda b: (0, 0, 0)),
                  pl.BlockSpec((1, 1, hk), lambda b: (b, 0, 0)),
                  pl.BlockSpec((1, 1, hk), lambda b: (b, 0, 1)),
                  pl.BlockSpec((1, 1, hv), lambda b: (b, 0, 0)),
                  pl.BlockSpec((1, 1, hv), lambda b: (b, 0, 0)),
                  pl.BlockSpec((1, hv), lambda b: (0, 0)),
                  pl.BlockSpec((1, 1, heads, RET_DK, RET_DV), lambda b: (layer, b, 0, 0, 0))],
        out_specs=pl.BlockSpec((1, 1, hv), lambda b: (b, 0, 0)),
        out_shape=jax.ShapeDtypeStruct((bs, 1, hv), BF),
        compiler_params=_cparams(("parallel",)),
        name=name,
    )(lg_tab, q3, q3, v3, g3, gn_g, state_all)
    return out.reshape(bs, hv)


def _ret_new_state_kernel(lg_ref, k_ref, v_ref, s0_ref, st_ref, *, heads):
    krow = k_ref[0, 0]
    vrow = v_ref[0, 0]
    for h in range(heads):
        gamma = jnp.exp(lg_ref[h][:, :1])
        kh = krow[:, h * RET_DK:(h + 1) * RET_DK]
        vh = vrow[:, h * RET_DV:(h + 1) * RET_DV]
        k_col = jnp.broadcast_to(kh, (SUBLANES, RET_DK)).T[:, :1]
        st_ref[0, 0, h] = gamma * s0_ref[0, 0, h] + k_col * vh


def _ret_new_state(lg_tab, k_all, v_all, state_all, *, heads, name):
    n_layers, bs = state_all.shape[:2]
    hk = heads * RET_DK
    hv = heads * RET_DV
    kern = functools.partial(_ret_new_state_kernel, heads=heads)
    st_spec = pl.BlockSpec((1, 1, heads, RET_DK, RET_DV), lambda l, b: (l, b, 0, 0, 0))
    return pl.pallas_call(
        kern,
        grid=(n_layers, bs),
        in_specs=[pl.BlockSpec(lg_tab.shape, lambda l, b: (0, 0, 0)),
                  pl.BlockSpec((1, 1, 1, hk), lambda l, b: (l, b, 0, 0)),
                  pl.BlockSpec((1, 1, 1, hv), lambda l, b: (l, b, 0, 0)),
                  st_spec],
        out_specs=st_spec,
        out_shape=jax.ShapeDtypeStruct(state_all.shape, F32),
        compiler_params=_cparams(("parallel", "parallel")),
        name=name,
    )(lg_tab, k_all, v_all, state_all)


def _diff_lambda(lq1_ref, lk1_ref, lq2_ref, lk2_ref, lam_init):
    a = jnp.sum(lq1_ref[...] * lk1_ref[...], axis=-1, keepdims=True)
    b = jnp.sum(lq2_ref[...] * lk2_ref[...], axis=-1, keepdims=True)
    return jnp.exp(a) - jnp.exp(b) + lam_init


def _diff_flash_kernel(qi_ref, kj_ref, q_ref, k_ref, v_ref, lq1_ref, lk1_ref, lq2_ref, lk2_ref, sg_ref,
                       o_ref, q1_ref, q2_ref, m1_ref, m2_ref, l1_ref, l2_ref, a1_ref, a2_ref,
                       *, tq, tk, rb_full, rb_diag, lam_init):
    step = pl.program_id(2)
    qi = qi_ref[step]
    kj = kj_ref[step]
    last_kj = ((qi + 1) * tq - 1) // tk
    comps = ((q1_ref, m1_ref, l1_ref, a1_ref), (q2_ref, m2_ref, l2_ref, a2_ref))

    @pl.when(kj == 0)
    def _():
        q = q_ref[...]
        lane = lax.broadcasted_iota(jnp.int32, q.shape, 1)
        zero = jnp.zeros_like(q)
        q1_ref[...] = jnp.where(lane < DIFF_D, q, zero)
        q2_ref[...] = jnp.where(lane >= DIFF_D, q, zero)
        for _, m_ref, l_ref, a_ref in comps:
            m_ref[...] = jnp.full_like(m_ref, -jnp.inf)
            l_ref[...] = jnp.zeros_like(l_ref)
            a_ref[...] = jnp.zeros_like(a_ref)

    def update(masked):
        rb = rb_diag if masked else rb_full
        k = k_ref[...]
        v = v_ref[...]
        v_ext = jnp.concatenate([v, jnp.ones_like(v)], axis=1)
        tri = (lax.broadcasted_iota(jnp.int32, (rb, rb), 1) <= lax.broadcasted_iota(jnp.int32, (rb, rb), 0))
        def chain(qc_ref, m_ref, l_ref, a_ref, r):
            rows = pl.ds(r * rb, rb)
            q_r = qc_ref[rows, :]
            nt = (((1,), (1,)), ((), ()))
            if masked and tq == tk:
                lo = r * rb
                parts = []
                if lo > 0:
                    parts.append((lax.dot_general(q_r, k[:lo], nt, preferred_element_type=F32), v_ext[:lo]))
                s_d = lax.dot_general(q_r, k[lo:lo + rb], nt, preferred_element_type=F32)
                parts.append((jnp.where(tri, s_d, -jnp.inf), v_ext[lo:lo + rb]))
            else:
                s = lax.dot_general(q_r, k, nt, preferred_element_type=F32)
                if masked:
                    row = lax.broadcasted_iota(jnp.int32, (rb, tk), 0) + (qi * tq + r * rb)
                    col = lax.broadcasted_iota(jnp.int32, (rb, tk), 1) + kj * tk
                    s = jnp.where(col <= row, s, -jnp.inf)
                parts = [(s, v_ext)]
            m_prev = m_ref[rows, :]
            m_new = m_prev
            for s, _ in parts:
                m_new = jnp.maximum(m_new, jnp.max(s, axis=-1, keepdims=True))
            alpha = jnp.exp2(m_prev - m_new)
            pv = None
            for s, vx in parts:
                p = jnp.exp2(s - jnp.tile(m_new, (1, s.shape[1] // LANES)))
                d = jnp.dot(p.astype(BF), vx, preferred_element_type=F32)
                pv = d if pv is None else pv + d
            a_ref[rows, :] = alpha * a_ref[rows, :] + pv[:, :LANES]
            l_ref[rows, :] = alpha * l_ref[rows, :] + pv[:, LANES:]
            m_ref[rows, :] = m_new

        for qc_ref, m_ref, l_ref, a_ref in comps:
            for r in range(tq // rb):
                chain(qc_ref, m_ref, l_ref, a_ref, r)


    straddles = (kj + 1) * tk - 1 > qi * tq

    @pl.when(straddles)
    def _():
        update(True)

    @pl.when(jnp.logical_not(straddles))
    def _():
        update(False)

    @pl.when(kj == last_kj)
    def _():
        lam = _diff_lambda(lq1_ref, lk1_ref, lq2_ref, lk2_ref, lam_init)
        o = a1_ref[...] / l1_ref[...] - lam * (a2_ref[...] / l2_ref[...])
        o = o * lax.rsqrt(jnp.mean(o * o, axis=-1, keepdims=True) + LN_EPS)
        o_ref[...] = (o * sg_ref[...] * (1.0 - lam_init)).astype(o_ref.dtype)


def _diff_flash(q, k, v, lq1, lk1, lq2, lk2, subln_g, *, batch, heads, tq, tk, rb_full, rb_diag, lam_init, name):
    m = q.shape[0]
    t = m // batch
    nq, nk = t // tq, t // tk
    qi_list, kj_list = [], []
    for i in range(nq):
        for j in range(((i + 1) * tq - 1) // tk + 1):
            qi_list.append(i)
            kj_list.append(j)
    qi_tab = jnp.asarray(np.array(qi_list, np.int32))
    kj_tab = jnp.asarray(np.array(kj_list, np.int32))
    hd = 2 * DIFF_D
    assert hd == LANES
    kern = functools.partial(_diff_flash_kernel, tq=tq, tk=tk, rb_full=rb_full, rb_diag=rb_diag, lam_init=lam_init)
    vec_spec = pl.BlockSpec((1, DIFF_D), lambda b, h, s, qi, kj: (0, 0))
    grid_spec = pltpu.PrefetchScalarGridSpec(
        num_scalar_prefetch=2,
        grid=(batch, heads, len(qi_list)),
        in_specs=[pl.BlockSpec((tq, hd), lambda b, h, s, qi, kj: (b * nq + qi[s], h)),
                  pl.BlockSpec((tk, hd), lambda b, h, s, qi, kj: (b * nk + kj[s], h)),
                  pl.BlockSpec((tk, hd), lambda b, h, s, qi, kj: (b * nk + kj[s], h)),
                  vec_spec, vec_spec, vec_spec, vec_spec,
                  pl.BlockSpec((1, hd), lambda b, h, s, qi, kj: (0, h))],
        out_specs=pl.BlockSpec((tq, hd), lambda b, h, s, qi, kj: (b * nq + qi[s], h)),
        scratch_shapes=[pltpu.VMEM((tq, hd), BF), pltpu.VMEM((tq, hd), BF)]
                       + [pltpu.VMEM((tq, hd), F32) for _ in range(6)],
    )
    return pl.pallas_call(
        kern,
        grid_spec=grid_spec,
        out_shape=jax.ShapeDtypeStruct((m, heads * hd), BF),
        compiler_params=_cparams(("parallel", "parallel", "arbitrary")),
        name=name,
    )(qi_tab, kj_tab, q, k, v, lq1, lk1, lq2, lk2, subln_g)


def _diff_decode_kernel(pt_ref, q_ref, kn_ref, vn_ref, lq1_ref, lk1_ref, lq2_ref, lk2_ref, sg_ref, *rest,
                        heads, n_fetch, lam_init):
    del pt_ref
    k_refs = rest[:n_fetch]
    v_refs = rest[n_fetch:2 * n_fetch]
    o_ref, m_ref, l_ref, acc_ref = rest[2 * n_fetch:]
    p_idx = pl.program_id(1)
    rows = 2 * heads

    @pl.when(p_idx == 0)
    def _():
        m_ref[...] = jnp.full_like(m_ref, -jnp.inf)
        l_ref[...] = jnp.zeros_like(l_ref)
        acc_ref[...] = jnp.zeros_like(acc_ref)

    q8 = q_ref[0]
    lane = lax.broadcasted_iota(jnp.int32, q8.shape, 1)
    q16 = jnp.concatenate([jnp.where(lane < DIFF_D, q8, 0.0),
                           jnp.where(lane >= DIFF_D, q8, 0.0)], axis=0)
    q16b = q16.astype(BF)

    ncol = k_refs[0].shape[1]
    r_head = lax.broadcasted_iota(jnp.int32, (rows, ncol), 0) % heads
    c_head = lax.broadcasted_iota(jnp.int32, (rows, ncol), 1) % heads
    same_head = r_head == c_head
    s_pages = []
    for k_ref in k_refs:
        s = lax.dot_general(q16b, k_ref[0].astype(BF), (((1,), (1,)), ((), ())),
                            preferred_element_type=F32)
        s_pages.append(jnp.where(same_head, s, -jnp.inf))
    m_prev = m_ref[...]
    m_new = m_prev
    for s in s_pages:
        m_new = jnp.maximum(m_new, jnp.max(s, axis=-1, keepdims=True))
    alpha = jnp.exp(m_prev - m_new)
    l_new = alpha * l_ref[...]
    acc = alpha * acc_ref[...]
    for s, v_ref in zip(s_pages, v_refs):
        p = jnp.exp(s - m_new)
        l_new = l_new + jnp.sum(p, axis=-1, keepdims=True)
        acc = acc + jnp.dot(p.astype(BF), v_ref[0].astype(BF), preferred_element_type=F32)
    l_ref[...] = l_new
    acc_ref[...] = acc
    m_ref[...] = m_new

    @pl.when(p_idx == pl.num_programs(1) - 1)
    def _():
        kn = kn_ref[0]
        vn = vn_ref[0]
        kn2 = jnp.concatenate([kn, kn], axis=0)
        vn2 = jnp.concatenate([vn, vn], axis=0)
        s_n = jnp.sum(q16 * kn2, axis=-1, keepdims=True)
        m_fin = jnp.maximum(m_new, s_n)
        beta = jnp.exp(m_new - m_fin)
        p_n = jnp.exp(s_n - m_fin)
        l_fin = beta * l_new + p_n
        acc_fin = beta * acc + p_n * vn2
        lam = _diff_lambda(lq1_ref, lk1_ref, lq2_ref, lk2_ref, lam_init)
        o = acc_fin[:heads] / l_fin[:heads] - lam * (acc_fin[heads:] / l_fin[heads:])
        o = o * lax.rsqrt(jnp.mean(o * o, axis=-1, keepdims=True) + LN_EPS)
        o_ref[0] = (o * sg_ref[...] * (1.0 - lam_init)).astype(o_ref.dtype)


def _diff_decode(page_table, q, kn, vn, lq1, lk1, lq2, lk2, subln_g, cache_k, cache_v, layer, *,
                 heads, lam_init, name):
    bs, n_pages = page_table.shape
    hd = 2 * DIFF_D
    n_layers, n_phys = cache_k.shape[:2]
    ck = cache_k.reshape(n_layers * n_phys, PAGE_SIZE * heads, hd)
    cv = cache_v.reshape(n_layers * n_phys, PAGE_SIZE * heads, hd)
    q3 = q.reshape(bs, heads, hd)
    kn3 = kn.reshape(bs, heads, hd)
    vn3 = vn.reshape(bs, heads, hd)
    sg = subln_g.reshape(heads, hd)
    pt = page_table.reshape(-1)
    n_fetch = DECODE_PAGES if n_pages % DECODE_PAGES == 0 else 1
    base = layer * n_phys
    kern = functools.partial(_diff_decode_kernel, heads=heads, n_fetch=n_fetch, lam_init=lam_init)
    row_spec = pl.BlockSpec((1, heads, hd), lambda b, p, pt: (b, 0, 0))
    vec_spec = pl.BlockSpec((1, DIFF_D), lambda b, p, pt: (0, 0))

    def page_spec(r):
        return pl.BlockSpec((1, PAGE_SIZE * heads, hd),
                            lambda b, p, pt: (base + pt[b * n_pages + p * n_fetch + r], 0, 0))

    page_specs = [page_spec(r) for r in range(n_fetch)]
    grid_spec = pltpu.PrefetchScalarGridSpec(
        num_scalar_prefetch=1,
        grid=(bs, n_pages // n_fetch),
        in_specs=[row_spec, row_spec, row_spec, vec_spec, vec_spec, vec_spec, vec_spec,
                  pl.BlockSpec((heads, hd), lambda b, p, pt: (0, 0))] + page_specs + page_specs,
        out_specs=pl.BlockSpec((1, heads, hd), lambda b, p, pt: (b, 0, 0)),
        scratch_shapes=[pltpu.VMEM((2 * heads, 1), F32),
                        pltpu.VMEM((2 * heads, 1), F32),
                        pltpu.VMEM((2 * heads, hd), F32)],
    )
    out = pl.pallas_call(
        kern,
        grid_spec=grid_spec,
        out_shape=jax.ShapeDtypeStruct((bs, heads, hd), BF),
        compiler_params=_cparams(("parallel", "arbitrary")),
        name=name,
    )(pt, q3, kn3, vn3, lq1, lk1, lq2, lk2, sg, *([ck] * n_fetch), *([cv] * n_fetch))
    return out.reshape(bs, heads * hd)


def _ret_tables(pos):
    inv = 1.0 / (RET_ROPE_BASE ** jnp.linspace(0.0, 1.0, RET_DK // 2, dtype=F32))
    ang = pos.astype(F32)[:, None] * inv[None, :]
    return jnp.cos(ang), jnp.sin(ang)


def _ret_tables_interleaved(pos, heads):
    cos, sin = _ret_tables(pos)
    zero = jnp.zeros_like(sin)
    c = jnp.stack([cos, cos], axis=-1).reshape(pos.shape[0], RET_DK)
    a = jnp.stack([-sin, zero], axis=-1).reshape(pos.shape[0], RET_DK)
    b = jnp.stack([zero, sin], axis=-1).reshape(pos.shape[0], RET_DK)
    k_scale = RET_DK ** -0.5
    return tuple(jnp.concatenate([jnp.tile(tab, (1, heads)), jnp.tile(tab * k_scale, (1, heads))], axis=1)
                 for tab in (c, a, b))


def _diff_tables(pos):
    half = DIFF_ROT // 2
    inv = 1.0 / (ROPE_THETA ** (jnp.arange(half, dtype=F32) * 2.0 / DIFF_ROT))
    ang = pos.astype(F32)[:, None] * inv[None, :]
    cos, sin = jnp.cos(ang), jnp.sin(ang)
    n = pos.shape[0]
    ones = jnp.ones((n, DIFF_D - DIFF_ROT), F32)
    zeros_rest = jnp.zeros((n, DIFF_D - DIFF_ROT), F32)
    zeros_h = jnp.zeros((n, half), F32)
    c = jnp.concatenate([cos, cos, ones], axis=1)
    a = jnp.concatenate([-sin, zeros_h, zeros_rest], axis=1)
    b = jnp.concatenate([zeros_h, sin, zeros_rest], axis=1)
    rep = LANES // DIFF_D
    return tuple(jnp.tile(tab, (1, rep)) for tab in (c, a, b))


def _ret_split_weights(w_in, heads):
    hk, hv = heads * RET_DK, heads * RET_DV
    perm = np.concatenate([np.arange(0, RET_DK, 2), np.arange(1, RET_DK, 2)])
    cols = np.concatenate([h * RET_DK + perm for h in range(heads)])
    w_q = w_in[:, :hk][:, cols]
    w_k = w_in[:, hk:2 * hk][:, cols]
    w_qk = jnp.concatenate([w_q, w_k], axis=1).astype(BF)
    w_v = w_in[:, 2 * hk:2 * hk + hv].astype(BF)
    w_g = w_in[:, 2 * hk + hv:].astype(BF)
    return w_qk, w_v, w_g


def kernel(x_prompt, x_sample, state_ret, cache_k, cache_v, page_table, ret_w_in, ret_gn_g, ret_w_o, diff_w_in, diff_lq1, diff_lk1, diff_lq2, diff_lk2, diff_subln_g, diff_w_o, ffn_w_in, ffn_w_down, ln1_g, ln1_b, ln2_g, ln2_b):
    bp, tp, d_model = x_prompt.shape
    bs, ts, _ = x_sample.shape
    assert ts == 1
    depth = ffn_w_in.shape[0]
    ret_heads = ret_w_o.shape[1] // RET_DV
    diff_heads = diff_w_o.shape[1] // (2 * DIFF_D)
    hd = diff_heads * 2 * DIFF_D
    past_len = page_table.shape[1] * PAGE_SIZE
    alpha = (2.0 * depth) ** 0.25
    mp = bp * tp

    pos_p = jnp.arange(tp)
    pos_s = jnp.broadcast_to(past_len + jnp.arange(ts), (bs,))
    ret_tab_p = _ret_tables(pos_p)
    ret_tab_s = _ret_tables_interleaved(pos_s, ret_heads)
    diff_tab_p = _diff_tables(pos_p)
    diff_tab_s = _diff_tables(pos_s)

    ret_chunk = RET_CHUNK if tp % RET_CHUNK == 0 else tp
    lg = jnp.log(1.0 - 2.0 ** (-5.0 - jnp.arange(ret_heads, dtype=F32)))
    lg_tab = jnp.broadcast_to(lg[:, None, None], (ret_heads, 1, max(ret_chunk, LANES)))

    tm_p = _row_tile(mp, PROJ_ROWS)
    tm_s = bs

    xp = x_prompt.reshape(mp, d_model)
    xs = x_sample.reshape(bs * ts, d_model)
    xp_b, xs_b = xp, xs

    n_diff = diff_w_in.shape[0]
    ret_p, kc_s, vc_s, ret_k_s, ret_v_s = [], [], [], [], []
    kc_p_all = vc_p_all = None
    for i in range(depth):
        j = i // N_MIXERS
        g1, b1 = ln1_g[i][None, :], ln1_b[i][None, :]
        g2, b2 = ln2_g[i][None, :], ln2_b[i][None, :]
        if i % N_MIXERS == 0:
            w_qk, w_v, w_g = _ret_split_weights(ret_w_in[j], ret_heads)
            w_o = ret_w_o[j].astype(BF)
            gn = ret_gn_g[j][None, :]
            w_all = jnp.concatenate([w_qk, w_v, w_g], axis=1)
            qk, v, g = _ret_proj(xp_b, w_all, *ret_tab_p, heads=ret_heads, tm=tm_p, name=f"ret{j}_proj_p")
            mix_p, st_p = _ret_chunk(lg_tab, qk, v, g, gn, batch=bp, heads=ret_heads, chunk=ret_chunk,
                                     name=f"ret{j}_chunk_p")
            ret_p.append(st_p.reshape(bp, ret_heads, RET_DK, RET_DV))
            w_qk_nat = ret_w_in[j][:, :2 * ret_heads * RET_DK].astype(BF)
            (qk_s,) = _shift_rope_proj(xs_b, w_qk_nat, ret_tab_s, [F32], scale=1.0, shift=1, tm=tm_s, tn=PROJ_COLS,
                                       name=f"ret{j}_qk_s")
            (v_s,) = _proj_plain(xs_b, w_v, [F32], tm=tm_s, tn=PROJ_COLS, name=f"ret{j}_v_s")
            (g_s,) = _proj_plain(xs_b, w_g, [F32], tm=tm_s, tn=PROJ_COLS, name=f"ret{j}_g_s")
            mix_s = _ret_step(lg_tab, qk_s, v_s, g_s, gn, state_ret, j, heads=ret_heads, name=f"ret{j}_step_s")
            ret_k_s.append(qk_s[:, ret_heads * RET_DK:])
            ret_v_s.append(v_s)
        else:
            lam_init = 0.8 - 0.6 * math.exp(-0.3 * (i + 1))
            w_in = diff_w_in[j].astype(BF)
            w_q, w_k, w_v = w_in[:, :hd], w_in[:, hd:2 * hd], w_in[:, 2 * hd:]
            w_o = diff_w_o[j].astype(BF)
            lam_args = (diff_lq1[j][None, :], diff_lk1[j][None, :], diff_lq2[j][None, :], diff_lk2[j][None, :])
            sg = diff_subln_g[j][None, :]
            rot = DIFF_ROT // 2
            q, kc_p_all, k_b, vc_p_all, v_b = _diff_proj(xp_b, w_in, diff_tab_p, kc_p_all, vc_p_all, j, n_diff,
                                                         q_scale=DIFF_D ** -0.5 * math.log2(math.e), shift=rot,
                                                         tm=tm_p, name=f"diff{j}_proj_p")
            tq = FLASH_TQ if tp % FLASH_TQ == 0 else tp
            tk = FLASH_TK if tp % FLASH_TK == 0 else tp
            mix_p = _diff_flash(q, k_b, v_b, *lam_args, sg, batch=bp, heads=diff_heads, tq=tq, tk=tk,
                                rb_full=min(FLASH_ROWS, tq), rb_diag=min(FLASH_ROWS_DIAG, tq),
                                lam_init=lam_init, name=f"diff{j}_flash_p")
            (q_s,) = _shift_rope_proj(xs_b, w_q, diff_tab_s, [F32], scale=DIFF_D ** -0.5, shift=rot, tm=tm_s, tn=PROJ_COLS,
                                      name=f"diff{j}_q_s")
            (kn,) = _shift_rope_proj(xs_b, w_k, diff_tab_s, [F32], scale=1.0, shift=rot, tm=tm_s, tn=PROJ_COLS,
                                     name=f"diff{j}_k_s")
            (vn,) = _proj_plain(xs_b, w_v, [F32], tm=tm_s, tn=PROJ_COLS, name=f"diff{j}_v_s")
            mix_s = _diff_decode(page_table, q_s, kn, vn, *lam_args, sg, cache_k, cache_v, j,
                                 heads=diff_heads, lam_init=lam_init, name=f"diff{j}_decode_s")
            kc_s.append(kn.reshape(bs, ts, diff_heads, 2 * DIFF_D))
            vc_s.append(vn.reshape(bs, ts, diff_heads, 2 * DIFF_D))

        w_fi = ffn_w_in[i].astype(BF)
        w_fd = ffn_w_down[i].astype(BF)
        xp, xp_b = _mix_ffn(mix_p, w_o, xp, g1, b1, w_fi, w_fd, g2, b2, alpha=alpha, tm=_row_tile(mp, FFN_ROWS),
                            th=FFN_COLS, name=f"l{i}_mix_ffn_p")
        xs, xs_b = _mix_ffn(mix_s, w_o, xs, g1, b1, w_fi, w_fd, g2, b2, alpha=alpha, tm=tm_s, th=FFN_COLS,
                            name=f"l{i}_mix_ffn_s")

    ret_s_all = _ret_new_state(lg_tab, jnp.stack(ret_k_s)[:, :, None, :], jnp.stack(ret_v_s)[:, :, None, :],
                               state_ret, heads=ret_heads, name="ret_new_state_s")
    return (xp.reshape(bp, tp, d_model), xs.reshape(bs, ts, d_model),
            jnp.stack(ret_p),
            kc_p_all.reshape(n_diff, bp, tp, diff_heads, 2 * DIFF_D),
            vc_p_all.reshape(n_diff, bp, tp, diff_heads, 2 * DIFF_D),
            ret_s_all, jnp.stack(kc_s), jnp.stack(vc_s))
```

```python
import functools
import math

import numpy as np
import jax
import jax.numpy as jnp
from jax import lax
from jax.experimental import pallas as pl
from jax.experimental.pallas import tpu as pltpu

BF = jnp.bfloat16
F32 = jnp.float32

RET_DK = 256
RET_DV = 512
RET_ROPE_BASE = 10000.0
DIFF_D = 64
DIFF_ROT = DIFF_D // 4
ROPE_THETA = 500000.0
PAGE_SIZE = 128
LN_EPS = 1e-5
N_MIXERS = 2

LANES = 128
SUBLANES = 8
PROJ_ROWS = 512
PROJ_COLS = 512
FFN_COLS = 256
NORM_ROWS = 256
FLASH_TQ = 2048
FLASH_TK = 2048
FLASH_ROWS = 256
FLASH_ROWS_DIAG = 512
RET_CHUNK = 512
FFN_ROWS = 512
DECODE_PAGES = 8
VMEM_LIMIT = 48 * 1024 * 1024


def _cparams(sem):
    return pltpu.CompilerParams(dimension_semantics=sem, vmem_limit_bytes=VMEM_LIMIT)


def _sigmoid(x):
    return 1.0 / (1.0 + jnp.exp(-x))


def _layer_norm(y, g, b):
    mu = jnp.mean(y, axis=-1, keepdims=True)
    d = y - mu
    var = jnp.mean(d * d, axis=-1, keepdims=True)
    return d * lax.rsqrt(var + LN_EPS) * g + b


def _row_tile(m, pref):
    return pref if m % pref == 0 else m


def _proj_plain_kernel(x_ref, w_ref, *o_refs):
    acc = jnp.dot(x_ref[...].astype(BF), w_ref[...], preferred_element_type=F32)
    for o_ref in o_refs:
        o_ref[...] = acc.astype(o_ref.dtype)


def _stacked_cols_spec(k, tn, layer, col0):
    assert col0 % tn == 0
    return pl.BlockSpec((None, k, tn), lambda i, j: (layer, 0, col0 // tn + j))


def _proj_plain(x, w, out_dtypes, *, layer, col0, n, tm, tn, name):
    m, k = x.shape
    return pl.pallas_call(
        _proj_plain_kernel,
        grid=(m // tm, n // tn),
        in_specs=[pl.BlockSpec((tm, k), lambda i, j: (i, 0)),
                  _stacked_cols_spec(k, tn, layer, col0)],
        out_specs=[pl.BlockSpec((tm, tn), lambda i, j: (i, j)) for _ in out_dtypes],
        out_shape=[jax.ShapeDtypeStruct((m, n), dt) for dt in out_dtypes],
        compiler_params=_cparams(("parallel", "parallel")),
        name=name,
    )(x, w)


def _shift_rope_kernel(x_ref, w_ref, c_ref, a_ref, b_ref, *o_refs, scale, shift):
    acc = jnp.dot(x_ref[...].astype(BF), w_ref[...], preferred_element_type=F32)
    tab_tiles = c_ref.shape[1] // LANES
    for t in range(acc.shape[1] // LANES):
        tt = t % tab_tiles
        cols = slice(tt * LANES, (tt + 1) * LANES)
        xs = acc[:, t * LANES:(t + 1) * LANES]
        up = pltpu.roll(xs, LANES - shift, axis=1)
        dn = pltpu.roll(xs, shift, axis=1)
        r = xs * c_ref[:, cols] + up * a_ref[:, cols] + dn * b_ref[:, cols]
        if scale != 1.0:
            r = r * scale
        for o_ref in o_refs:
            o_ref[:, t * LANES:(t + 1) * LANES] = r.astype(o_ref.dtype)


def _shift_rope_proj(x, w, tabs, out_dtypes, *, layer, col0, n, scale, shift, tm, tn, name):
    m, k = x.shape
    ntab = tabs[0].shape[0] // tm
    tw = min(tabs[0].shape[1], tn)
    ntab_cols = tabs[0].shape[1] // tw
    kern = functools.partial(_shift_rope_kernel, scale=scale, shift=shift)
    tab_spec = pl.BlockSpec((tm, tw), lambda i, j: (i % ntab, j % ntab_cols))
    return pl.pallas_call(
        kern,
        grid=(m // tm, n // tn),
        in_specs=[pl.BlockSpec((tm, k), lambda i, j: (i, 0)),
                  _stacked_cols_spec(k, tn, layer, col0),
                  tab_spec, tab_spec, tab_spec],
        out_specs=[pl.BlockSpec((tm, tn), lambda i, j: (i, j)) for _ in out_dtypes],
        out_shape=[jax.ShapeDtypeStruct((m, n), dt) for dt in out_dtypes],
        compiler_params=_cparams(("parallel", "parallel")),
        name=name,
    )(x, w, *tabs)


def _resident(block_shape, index_map):
    return pl.BlockSpec(block_shape, index_map, pipeline_mode=pl.Buffered(1))


def _ret_proj_kernel(x_ref, w_ref, cos_ref, sin_ref, qk_ref, v_ref, g_ref, *, heads, tn):
    xb = x_ref[...].astype(BF)
    hk, hv = heads * RET_DK, heads * RET_DV
    half = RET_DK // 2
    c = cos_ref[...]
    s = sin_ref[...]
    for h in range(2 * heads):
        acc = jnp.dot(xb, w_ref[:, h * RET_DK:(h + 1) * RET_DK], preferred_element_type=F32)
        x1 = acc[:, :half]
        x2 = acc[:, half:]
        r1 = x1 * c - x2 * s
        r2 = x2 * c + x1 * s
        if h >= heads:
            r1 = r1 * RET_DK ** -0.5
            r2 = r2 * RET_DK ** -0.5
        qk_ref[:, h * RET_DK:h * RET_DK + half] = r1.astype(qk_ref.dtype)
        qk_ref[:, h * RET_DK + half:(h + 1) * RET_DK] = r2.astype(qk_ref.dtype)
    for t in range(hv // tn):
        cols = slice(t * tn, (t + 1) * tn)
        v_ref[:, cols] = jnp.dot(xb, w_ref[:, 2 * hk + t * tn:2 * hk + (t + 1) * tn],
                                 preferred_element_type=F32).astype(v_ref.dtype)
    for t in range(hv // tn):
        cols = slice(t * tn, (t + 1) * tn)
        g_ref[:, cols] = jnp.dot(xb, w_ref[:, 2 * hk + hv + t * tn:2 * hk + hv + (t + 1) * tn],
                                 preferred_element_type=F32).astype(g_ref.dtype)


def _ret_proj(x, w_all, cos, sin, layer, *, heads, tm, name):
    m, k = x.shape
    hk, hv = heads * RET_DK, heads * RET_DV
    ntab = cos.shape[0] // tm
    kern = functools.partial(_ret_proj_kernel, heads=heads, tn=PROJ_COLS)
    tab_spec = pl.BlockSpec((tm, RET_DK // 2), lambda i: (i % ntab, 0))
    return pl.pallas_call(
        kern,
        grid=(m // tm,),
        in_specs=[pl.BlockSpec((tm, k), lambda i: (i, 0)),
                  _resident((None, k, 2 * hk + 2 * hv), lambda i: (layer, 0, 0)),
                  tab_spec, tab_spec],
        out_specs=[pl.BlockSpec((tm, 2 * hk), lambda i: (i, 0)),
                   pl.BlockSpec((tm, hv), lambda i: (i, 0)),
                   pl.BlockSpec((tm, hv), lambda i: (i, 0))],
        out_shape=[jax.ShapeDtypeStruct((m, 2 * hk), BF),
                   jax.ShapeDtypeStruct((m, hv), BF),
                   jax.ShapeDtypeStruct((m, hv), F32)],
        compiler_params=_cparams(("parallel",)),
        name=name,
    )(x, w_all, cos, sin)


def _diff_proj_kernel(*refs, hd, q_scale, shift, tn, layer, first):
    if first:
        x_ref, w_ref, c_ref, a_ref, b_ref, q_ref, kf_ref, kb_ref, vf_ref, vb_ref = refs
        for l in range(kf_ref.shape[0]):
            if l != layer:
                kf_ref[l] = jnp.zeros(kf_ref.shape[1:], F32)
                vf_ref[l] = jnp.zeros(vf_ref.shape[1:], F32)
        kf_dst, vf_dst = kf_ref.at[layer], vf_ref.at[layer]
    else:
        x_ref, w_ref, c_ref, a_ref, b_ref, _, _, q_ref, kf_dst, kb_ref, vf_dst, vb_ref = refs
    xb = x_ref[...].astype(BF)
    c = c_ref[...]
    a = a_ref[...]
    b = b_ref[...]

    def rope(acc, scale, col0, o_refs):
        for t in range(acc.shape[1] // LANES):
            xs = acc[:, t * LANES:(t + 1) * LANES]
            up = pltpu.roll(xs, LANES - shift, axis=1)
            dn = pltpu.roll(xs, shift, axis=1)
            r = xs * c + up * a + dn * b
            if scale != 1.0:
                r = r * scale
            for o_ref in o_refs:
                o_ref[:, col0 + t * LANES:col0 + (t + 1) * LANES] = r.astype(o_ref.dtype)

    for t in range(hd // tn):
        acc = jnp.dot(xb, w_ref[:, t * tn:(t + 1) * tn], preferred_element_type=F32)
        rope(acc, q_scale, t * tn, (q_ref,))
    for t in range(hd // tn):
        acc = jnp.dot(xb, w_ref[:, hd + t * tn:hd + (t + 1) * tn], preferred_element_type=F32)
        rope(acc, 1.0, t * tn, (kf_dst, kb_ref))
    for t in range(hd // tn):
        cols = slice(t * tn, (t + 1) * tn)
        acc = jnp.dot(xb, w_ref[:, 2 * hd + t * tn:2 * hd + (t + 1) * tn], preferred_element_type=F32)
        vf_dst[:, cols] = acc
        vb_ref[:, cols] = acc.astype(BF)


def _diff_proj(x, w_all, tabs, kf_all, vf_all, layer, n_layers, *, q_scale, shift, tm, name):
    m, k = x.shape
    hd = w_all.shape[2] // 3
    ntab = tabs[0].shape[0] // tm
    first = kf_all is None
    kern = functools.partial(_diff_proj_kernel, hd=hd, q_scale=q_scale, shift=shift, tn=PROJ_COLS, layer=layer,
                             first=first)
    tab_spec = pl.BlockSpec((tm, LANES), lambda i: (i % ntab, 0))
    row_spec = pl.BlockSpec((tm, hd), lambda i: (i, 0))
    in_specs = [pl.BlockSpec((tm, k), lambda i: (i, 0)),
                _resident((None, k, 3 * hd), lambda i: (layer, 0, 0)),
                tab_spec, tab_spec, tab_spec]
    args = [x, w_all, *tabs]
    if first:
        stack_spec = pl.BlockSpec((n_layers, tm, hd), lambda i: (0, i, 0))
        aliases = {}
    else:
        stack_spec = pl.BlockSpec((None, tm, hd), lambda i: (layer, i, 0))
        in_specs += [pl.BlockSpec(memory_space=pl.ANY), pl.BlockSpec(memory_space=pl.ANY)]
        args += [kf_all, vf_all]
        aliases = {5: 1, 6: 3}
    stacked = jax.ShapeDtypeStruct((n_layers, m, hd), F32)
    return pl.pallas_call(
        kern,
        grid=(m // tm,),
        in_specs=in_specs,
        out_specs=[row_spec, stack_spec, row_spec, stack_spec, row_spec],
        out_shape=[jax.ShapeDtypeStruct((m, hd), BF), stacked, jax.ShapeDtypeStruct((m, hd), BF), stacked,
                   jax.ShapeDtypeStruct((m, hd), BF)],
        input_output_aliases=aliases,
        compiler_params=_cparams(("parallel",)),
        name=name,
    )(*args)


def _mix_ffn_kernel(a_ref, wo_ref, x_ref, g1_ref, b1_ref, wg_ref, wu_ref, wd_ref, g2_ref, b2_ref,
                    o_ref, ob_ref, *, alpha, th, rb):
    wo = wo_ref[...]
    x1_rows = []
    for r in range(a_ref.shape[0] // rb):
        rows = pl.ds(r * rb, rb)
        sub = jnp.dot(a_ref[rows, :].astype(BF), wo, preferred_element_type=F32)
        x1_rows.append(_layer_norm(alpha * x_ref[rows, :] + sub, g1_ref[...], b1_ref[...]))
    x1 = jnp.concatenate(x1_rows, axis=0)
    xb = x1.astype(BF)
    hidden = wd_ref.shape[0]
    hs = []
    for c in range(hidden // th):
        cols = slice(c * th, (c + 1) * th)
        gate = jnp.dot(xb, wg_ref[:, cols], preferred_element_type=F32)
        up = jnp.dot(xb, wu_ref[:, cols], preferred_element_type=F32)
        hs.append((gate * _sigmoid(gate) * up).astype(BF))
    h = jnp.concatenate(hs, axis=1)
    sub = jnp.dot(h, wd_ref[...], preferred_element_type=F32)
    y = _layer_norm(alpha * x1 + sub, g2_ref[...], b2_ref[...])
    o_ref[...] = y
    ob_ref[...] = y.astype(BF)


def _mix_ffn(a, w_o, mixer, x, g1, b1, w_in, w_down, layer, g2, b2, *, alpha, tm, th, name):
    m, ka = a.shape
    d = x.shape[1]
    hidden = w_down.shape[1]
    kern = functools.partial(_mix_ffn_kernel, alpha=alpha, th=th, rb=min(tm, NORM_ROWS))
    vec = pl.BlockSpec((1, d), lambda i: (0, 0))
    return pl.pallas_call(
        kern,
        grid=(m // tm,),
        in_specs=[pl.BlockSpec((tm, ka), lambda i: (i, 0)),
                  _resident((None, ka, d), lambda i: (mixer, 0, 0)),
                  pl.BlockSpec((tm, d), lambda i: (i, 0)),
                  vec, vec,
                  _resident((None, d, hidden), lambda i: (layer, 0, 0)),
                  _resident((None, d, hidden), lambda i: (layer, 0, 1)),
                  _resident((None, hidden, d), lambda i: (layer, 0, 0)),
                  vec, vec],
        out_specs=[pl.BlockSpec((tm, d), lambda i: (i, 0)),
                   pl.BlockSpec((tm, d), lambda i: (i, 0))],
        out_shape=[jax.ShapeDtypeStruct((m, d), F32), jax.ShapeDtypeStruct((m, d), BF)],
        compiler_params=_cparams(("parallel",)),
        name=name,
    )(a, w_o, x, g1, b1, w_in, w_in, w_down, g2, b2)


def _ret_chunk_kernel(lg_ref, q_ref, k_ref, v_ref, g_ref, gn_ref, o_ref, st_ref, s_ref, decay_ref, *, chunk):
    c = pl.program_id(2)
    half = RET_DK // 2
    lg_row = lg_ref[0]
    lg = lg_row[:, :1]

    @pl.when(c == 0)
    def _():
        s_ref[...] = jnp.zeros_like(s_ref)
        ii = lax.broadcasted_iota(jnp.int32, (chunk, chunk), 0)
        jj = lax.broadcasted_iota(jnp.int32, (chunk, chunk), 1)
        rel = (ii - jj).astype(F32)
        decay_ref[...] = jnp.where(rel >= 0, jnp.exp(lg_row * jnp.maximum(rel, 0.0)), 0.0)

    idx = lax.broadcasted_iota(jnp.int32, (chunk, 1), 0).astype(F32)
    q_decay = jnp.exp(lg * (idx + 1.0))
    k_decay = jnp.exp(lg * (chunk - 1.0 - idx))
    chunk_decay = jnp.exp(lg * float(chunk))

    q = q_ref[...]
    k = k_ref[...]
    v = v_ref[...]
    s_old = s_ref[...]
    scores = lax.dot_general(q, k, (((1,), (1,)), ((), ())), preferred_element_type=F32) * decay_ref[...]
    inner = jnp.dot(scores.astype(BF), v, preferred_element_type=F32)
    cross = jnp.dot(q, s_old.astype(BF), preferred_element_type=F32) * q_decay
    kd_t = (k.astype(F32) * k_decay).T.astype(BF)
    s_ref[...] = chunk_decay * s_old + jnp.dot(kd_t, v, preferred_element_type=F32)

    o = inner + cross
    mu = jnp.mean(o, axis=-1, keepdims=True)
    d = o - mu
    var = jnp.mean(d * d, axis=-1, keepdims=True)
    on = d * lax.rsqrt(var + LN_EPS) * gn_ref[...]
    gt = g_ref[...].astype(F32)
    o_ref[...] = (gt * _sigmoid(gt) * on).astype(o_ref.dtype)

    @pl.when(c == pl.num_programs(2) - 1)
    def _():
        st_ref[0, 0, :, :RET_DV] = s_ref[:half, :]
        st_ref[0, 0, :, RET_DV:] = s_ref[half:, :]


def _ret_chunk(lg_tab, qk, v, g, gn_g, *, batch, heads, chunk, name):
    m = qk.shape[0]
    t = m // batch
    nc = t // chunk
    kern = functools.partial(_ret_chunk_kernel, chunk=chunk)
    return pl.pallas_call(
        kern,
        grid=(batch, heads, nc),
        in_specs=[pl.BlockSpec((1, 1, chunk), lambda b, h, c: (h, 0, 0)),
                  pl.BlockSpec((chunk, RET_DK), lambda b, h, c: (b * nc + c, h)),
                  pl.BlockSpec((chunk, RET_DK), lambda b, h, c: (b * nc + c, heads + h)),
                  pl.BlockSpec((chunk, RET_DV), lambda b, h, c: (b * nc + c, h)),
                  pl.BlockSpec((chunk, RET_DV), lambda b, h, c: (b * nc + c, h)),
                  pl.BlockSpec((1, RET_DV), lambda b, h, c: (0, h))],
        out_specs=[pl.BlockSpec((chunk, RET_DV), lambda b, h, c: (b * nc + c, h)),
                   pl.BlockSpec((1, 1, RET_DK // 2, 2 * RET_DV), lambda b, h, c: (b, h, 0, 0))],
        out_shape=[jax.ShapeDtypeStruct((m, heads * RET_DV), BF),
                   jax.ShapeDtypeStruct((batch, heads, RET_DK // 2, 2 * RET_DV), F32)],
        scratch_shapes=[pltpu.VMEM((RET_DK, RET_DV), F32), pltpu.VMEM((chunk, chunk), F32)],
        compiler_params=_cparams(("parallel", "parallel", "arbitrary")),
        name=name,
    )(lg_tab, qk, qk, v, g, gn_g)


def _ret_step_kernel(lg_ref, q_ref, k_ref, v_ref, g_ref, gn_ref, s0_ref, o_ref, *, heads):
    qrow = q_ref[0]
    krow = k_ref[0]
    vrow = v_ref[0]
    grow = g_ref[0]
    for h in range(heads):
        gamma = jnp.exp(lg_ref[h][:, :1])
        qh = qrow[:, h * RET_DK:(h + 1) * RET_DK]
        kh = krow[:, h * RET_DK:(h + 1) * RET_DK]
        vh = vrow[:, h * RET_DV:(h + 1) * RET_DV]
        s0 = s0_ref[0, 0, h]
        q8 = jnp.broadcast_to(qh, (SUBLANES, RET_DK)).astype(BF)
        cross = jnp.dot(q8, s0.astype(BF), preferred_element_type=F32)[:1]
        qk = jnp.sum(qh * kh, axis=-1, keepdims=True)
        o = qk * vh + gamma * cross
        mu = jnp.mean(o, axis=-1, keepdims=True)
        d = o - mu
        var = jnp.mean(d * d, axis=-1, keepdims=True)
        on = d * lax.rsqrt(var + LN_EPS) * gn_ref[:, h * RET_DV:(h + 1) * RET_DV]
        gt = grow[:, h * RET_DV:(h + 1) * RET_DV]
        o_ref[0, :, h * RET_DV:(h + 1) * RET_DV] = (gt * _sigmoid(gt) * on).astype(o_ref.dtype)


def _ret_step(lg_tab, qk, v, g, gn_g, state_all, layer, *, heads, name):
    bs = qk.shape[0]
    hk = heads * RET_DK
    hv = heads * RET_DV
    q3 = qk.reshape(bs, 1, 2 * hk)
    v3 = v.reshape(bs, 1, hv)
    g3 = g.reshape(bs, 1, hv)
    kern = functools.partial(_ret_step_kernel, heads=heads)
    out = pl.pallas_call(
        kern,
        grid=(bs,),
        in_specs=[pl.BlockSpec(lg_tab.shape, lambda b: (0, 0, 0)),
                  pl.BlockSpec((1, 1, hk), lambda b: (b, 0, 0)),
                  pl.BlockSpec((1, 1, hk), lambda b: (b, 0, 1)),
                  pl.BlockSpec((1, 1, hv), lambda b: (b, 0, 0)),
                  pl.BlockSpec((1, 1, hv), lambda b: (b, 0, 0)),
                  pl.BlockSpec((1, hv), lambda b: (0, 0)),
                  pl.BlockSpec((1, 1, heads, RET_DK, RET_DV), lambda b: (layer, b, 0, 0, 0))],
        out_specs=pl.BlockSpec((1, 1, hv), lambda b: (b, 0, 0)),
        out_shape=jax.ShapeDtypeStruct((bs, 1, hv), BF),
        compiler_params=_cparams(("parallel",)),
        name=name,
    )(lg_tab, q3, q3, v3, g3, gn_g, state_all)
    return out.reshape(bs, hv)


def _ret_new_state_kernel(lg_ref, k_ref, v_ref, s0_ref, st_ref, *, heads):
    krow = k_ref[0, 0]
    vrow = v_ref[0, 0]
    for h in range(heads):
        gamma = jnp.exp(lg_ref[h][:, :1])
        kh = krow[:, h * RET_DK:(h + 1) * RET_DK]
        vh = vrow[:, h * RET_DV:(h + 1) * RET_DV]
        k_col = jnp.broadcast_to(kh, (SUBLANES, RET_DK)).T[:, :1]
        st_ref[0, 0, h] = gamma * s0_ref[0, 0, h] + k_col * vh


def _ret_new_state(lg_tab, k_all, v_all, state_all, *, heads, name):
    n_layers, bs = state_all.shape[:2]
    hk = heads * RET_DK
    hv = heads * RET_DV
    kern = functools.partial(_ret_new_state_kernel, heads=heads)
    st_spec = pl.BlockSpec((1, 1, heads, RET_DK, RET_DV), lambda l, b: (l, b, 0, 0, 0))
    return pl.pallas_call(
        kern,
        grid=(n_layers, bs),
        in_specs=[pl.BlockSpec(lg_tab.shape, lambda l, b: (0, 0, 0)),
                  pl.BlockSpec((1, 1, 1, hk), lambda l, b: (l, b, 0, 0)),
                  pl.BlockSpec((1, 1, 1, hv), lambda l, b: (l, b, 0, 0)),
                  st_spec],
        out_specs=st_spec,
        out_shape=jax.ShapeDtypeStruct(state_all.shape, F32),
        compiler_params=_cparams(("parallel", "parallel")),
        name=name,
    )(lg_tab, k_all, v_all, state_all)


def _diff_lambda(lq1_ref, lk1_ref, lq2_ref, lk2_ref, lam_init):
    a = jnp.sum(lq1_ref[...] * lk1_ref[...], axis=-1, keepdims=True)
    b = jnp.sum(lq2_ref[...] * lk2_ref[...], axis=-1, keepdims=True)
    return jnp.exp(a) - jnp.exp(b) + lam_init


def _diff_flash_kernel(qi_ref, kj_ref, q_ref, k_ref, v_ref, lq1_ref, lk1_ref, lq2_ref, lk2_ref, sg_ref,
                       o_ref, q1_ref, q2_ref, m1_ref, m2_ref, l1_ref, l2_ref, a1_ref, a2_ref,
                       *, tq, tk, rb_full, rb_diag, lam_init):
    step = pl.program_id(2)
    qi = qi_ref[step]
    kj = kj_ref[step]
    last_kj = ((qi + 1) * tq - 1) // tk
    comps = ((q1_ref, m1_ref, l1_ref, a1_ref), (q2_ref, m2_ref, l2_ref, a2_ref))

    @pl.when(kj == 0)
    def _():
        q = q_ref[...]
        lane = lax.broadcasted_iota(jnp.int32, q.shape, 1)
        zero = jnp.zeros_like(q)
        q1_ref[...] = jnp.where(lane < DIFF_D, q, zero)
        q2_ref[...] = jnp.where(lane >= DIFF_D, q, zero)
        for _, m_ref, l_ref, a_ref in comps:
            m_ref[...] = jnp.full_like(m_ref, -jnp.inf)
            l_ref[...] = jnp.zeros_like(l_ref)
            a_ref[...] = jnp.zeros_like(a_ref)

    def update(masked):
        rb = rb_diag if masked else rb_full
        k = k_ref[...]
        v = v_ref[...]
        v_ext = jnp.concatenate([v, jnp.ones_like(v)], axis=1)
        tri = (lax.broadcasted_iota(jnp.int32, (rb, rb), 1) <= lax.broadcasted_iota(jnp.int32, (rb, rb), 0))
        def chain(qc_ref, m_ref, l_ref, a_ref, r):
            rows = pl.ds(r * rb, rb)
            q_r = qc_ref[rows, :]
            nt = (((1,), (1,)), ((), ()))
            if masked and tq == tk:
                lo = r * rb
                parts = []
                if lo > 0:
                    parts.append((lax.dot_general(q_r, k[:lo], nt, preferred_element_type=F32), v_ext[:lo]))
                s_d = lax.dot_general(q_r, k[lo:lo + rb], nt, preferred_element_type=F32)
                parts.append((jnp.where(tri, s_d, -jnp.inf), v_ext[lo:lo + rb]))
            else:
                s = lax.dot_general(q_r, k, nt, preferred_element_type=F32)
                if masked:
                    row = lax.broadcasted_iota(jnp.int32, (rb, tk), 0) + (qi * tq + r * rb)
                    col = lax.broadcasted_iota(jnp.int32, (rb, tk), 1) + kj * tk
                    s = jnp.where(col <= row, s, -jnp.inf)
                parts = [(s, v_ext)]
            m_prev = m_ref[rows, :]
            m_new = m_prev
            for s, _ in parts:
                m_new = jnp.maximum(m_new, jnp.max(s, axis=-1, keepdims=True))
            alpha = jnp.exp2(m_prev - m_new)
            pv = None
            for s, vx in parts:
                p = jnp.exp2(s - jnp.tile(m_new, (1, s.shape[1] // LANES)))
                d = jnp.dot(p.astype(BF), vx, preferred_element_type=F32)
                pv = d if pv is None else pv + d
            a_ref[rows, :] = alpha * a_ref[rows, :] + pv[:, :LANES]
            l_ref[rows, :] = alpha * l_ref[rows, :] + pv[:, LANES:]
            m_ref[rows, :] = m_new

        for qc_ref, m_ref, l_ref, a_ref in comps:
            for r in range(tq // rb):
                chain(qc_ref, m_ref, l_ref, a_ref, r)


    straddles = (kj + 1) * tk - 1 > qi * tq

    @pl.when(straddles)
    def _():
        update(True)

    @pl.when(jnp.logical_not(straddles))
    def _():
        update(False)

    @pl.when(kj == last_kj)
    def _():
        lam = _diff_lambda(lq1_ref, lk1_ref, lq2_ref, lk2_ref, lam_init)
        o = a1_ref[...] / l1_ref[...] - lam * (a2_ref[...] / l2_ref[...])
        o = o * lax.rsqrt(jnp.mean(o * o, axis=-1, keepdims=True) + LN_EPS)
        o_ref[...] = (o * sg_ref[...] * (1.0 - lam_init)).astype(o_ref.dtype)


def _diff_flash(q, k, v, lq1, lk1, lq2, lk2, subln_g, *, batch, heads, tq, tk, rb_full, rb_diag, lam_init, name):
    m = q.shape[0]
    t = m // batch
    nq, nk = t // tq, t // tk
    qi_list, kj_list = [], []
    for i in range(nq):
        for j in range(((i + 1) * tq - 1) // tk + 1):
            qi_list.append(i)
            kj_list.append(j)
    qi_tab = jnp.asarray(np.array(qi_list, np.int32))
    kj_tab = jnp.asarray(np.array(kj_list, np.int32))
    hd = 2 * DIFF_D
    assert hd == LANES
    kern = functools.partial(_diff_flash_kernel, tq=tq, tk=tk, rb_full=rb_full, rb_diag=rb_diag, lam_init=lam_init)
    vec_spec = pl.BlockSpec((1, DIFF_D), lambda b, h, s, qi, kj: (0, 0))
    grid_spec = pltpu.PrefetchScalarGridSpec(
        num_scalar_prefetch=2,
        grid=(batch, heads, len(qi_list)),
        in_specs=[pl.BlockSpec((tq, hd), lambda b, h, s, qi, kj: (b * nq + qi[s], h)),
                  pl.BlockSpec((tk, hd), lambda b, h, s, qi, kj: (b * nk + kj[s], h)),
                  pl.BlockSpec((tk, hd), lambda b, h, s, qi, kj: (b * nk + kj[s], h)),
                  vec_spec, vec_spec, vec_spec, vec_spec,
                  pl.BlockSpec((1, hd), lambda b, h, s, qi, kj: (0, h))],
        out_specs=pl.BlockSpec((tq, hd), lambda b, h, s, qi, kj: (b * nq + qi[s], h)),
        scratch_shapes=[pltpu.VMEM((tq, hd), BF), pltpu.VMEM((tq, hd), BF)]
                       + [pltpu.VMEM((tq, hd), F32) for _ in range(6)],
    )
    return pl.pallas_call(
        kern,
        grid_spec=grid_spec,
        out_shape=jax.ShapeDtypeStruct((m, heads * hd), BF),
        compiler_params=_cparams(("parallel", "parallel", "arbitrary")),
        name=name,
    )(qi_tab, kj_tab, q, k, v, lq1, lk1, lq2, lk2, subln_g)


def _diff_decode_kernel(pt_ref, q_ref, kn_ref, vn_ref, lq1_ref, lk1_ref, lq2_ref, lk2_ref, sg_ref, *rest,
                        heads, n_fetch, lam_init):
    del pt_ref
    k_refs = rest[:n_fetch]
    v_refs = rest[n_fetch:2 * n_fetch]
    o_ref, m_ref, l_ref, acc_ref = rest[2 * n_fetch:]
    p_idx = pl.program_id(1)
    rows = 2 * heads

    @pl.when(p_idx == 0)
    def _():
        m_ref[...] = jnp.full_like(m_ref, -jnp.inf)
        l_ref[...] = jnp.zeros_like(l_ref)
        acc_ref[...] = jnp.zeros_like(acc_ref)

    q8 = q_ref[0]
    lane = lax.broadcasted_iota(jnp.int32, q8.shape, 1)
    q16 = jnp.concatenate([jnp.where(lane < DIFF_D, q8, 0.0),
                           jnp.where(lane >= DIFF_D, q8, 0.0)], axis=0)
    q16b = q16.astype(BF)

    ncol = k_refs[0].shape[1]
    r_head = lax.broadcasted_iota(jnp.int32, (rows, ncol), 0) % heads
    c_head = lax.broadcasted_iota(jnp.int32, (rows, ncol), 1) % heads
    same_head = r_head == c_head
    s_pages = []
    for k_ref in k_refs:
        s = lax.dot_general(q16b, k_ref[0].astype(BF), (((1,), (1,)), ((), ())),
                            preferred_element_type=F32)
        s_pages.append(jnp.where(same_head, s, -jnp.inf))
    m_prev = m_ref[...]
    m_new = m_prev
    for s in s_pages:
        m_new = jnp.maximum(m_new, jnp.max(s, axis=-1, keepdims=True))
    alpha = jnp.exp(m_prev - m_new)
    l_new = alpha * l_ref[...]
    acc = alpha * acc_ref[...]
    for s, v_ref in zip(s_pages, v_refs):
        p = jnp.exp(s - m_new)
        l_new = l_new + jnp.sum(p, axis=-1, keepdims=True)
        acc = acc + jnp.dot(p.astype(BF), v_ref[0].astype(BF), preferred_element_type=F32)
    l_ref[...] = l_new
    acc_ref[...] = acc
    m_ref[...] = m_new

    @pl.when(p_idx == pl.num_programs(1) - 1)
    def _():
        kn = kn_ref[0]
        vn = vn_ref[0]
        kn2 = jnp.concatenate([kn, kn], axis=0)
        vn2 = jnp.concatenate([vn, vn], axis=0)
        s_n = jnp.sum(q16 * kn2, axis=-1, keepdims=True)
        m_fin = jnp.maximum(m_new, s_n)
        beta = jnp.exp(m_new - m_fin)
        p_n = jnp.exp(s_n - m_fin)
        l_fin = beta * l_new + p_n
        acc_fin = beta * acc + p_n * vn2
        lam = _diff_lambda(lq1_ref, lk1_ref, lq2_ref, lk2_ref, lam_init)
        o = acc_fin[:heads] / l_fin[:heads] - lam * (acc_fin[heads:] / l_fin[heads:])
        o = o * lax.rsqrt(jnp.mean(o * o, axis=-1, keepdims=True) + LN_EPS)
        o_ref[0] = (o * sg_ref[...] * (1.0 - lam_init)).astype(o_ref.dtype)


def _diff_decode(page_table, q, kn, vn, lq1, lk1, lq2, lk2, subln_g, cache_k, cache_v, layer, *,
                 heads, lam_init, name):
    bs, n_pages = page_table.shape
    hd = 2 * DIFF_D
    n_layers, n_phys = cache_k.shape[:2]
    ck = cache_k.reshape(n_layers * n_phys, PAGE_SIZE * heads, hd)
    cv = cache_v.reshape(n_layers * n_phys, PAGE_SIZE * heads, hd)
    q3 = q.reshape(bs, heads, hd)
    kn3 = kn.reshape(bs, heads, hd)
    vn3 = vn.reshape(bs, heads, hd)
    sg = subln_g.reshape(heads, hd)
    pt = page_table.reshape(-1)
    n_fetch = DECODE_PAGES if n_pages % DECODE_PAGES == 0 else 1
    base = layer * n_phys
    kern = functools.partial(_diff_decode_kernel, heads=heads, n_fetch=n_fetch, lam_init=lam_init)
    row_spec = pl.BlockSpec((1, heads, hd), lambda b, p, pt: (b, 0, 0))
    vec_spec = pl.BlockSpec((1, DIFF_D), lambda b, p, pt: (0, 0))

    def page_spec(r):
        return pl.BlockSpec((1, PAGE_SIZE * heads, hd),
                            lambda b, p, pt: (base + pt[b * n_pages + p * n_fetch + r], 0, 0))

    page_specs = [page_spec(r) for r in range(n_fetch)]
    grid_spec = pltpu.PrefetchScalarGridSpec(
        num_scalar_prefetch=1,
        grid=(bs, n_pages // n_fetch),
        in_specs=[row_spec, row_spec, row_spec, vec_spec, vec_spec, vec_spec, vec_spec,
                  pl.BlockSpec((heads, hd), lambda b, p, pt: (0, 0))] + page_specs + page_specs,
        out_specs=pl.BlockSpec((1, heads, hd), lambda b, p, pt: (b, 0, 0)),
        scratch_shapes=[pltpu.VMEM((2 * heads, 1), F32),
                        pltpu.VMEM((2 * heads, 1), F32),
                        pltpu.VMEM((2 * heads, hd), F32)],
    )
    out = pl.pallas_call(
        kern,
        grid_spec=grid_spec,
        out_shape=jax.ShapeDtypeStruct((bs, heads, hd), BF),
        compiler_params=_cparams(("parallel", "arbitrary")),
        name=name,
    )(pt, q3, kn3, vn3, lq1, lk1, lq2, lk2, sg, *([ck] * n_fetch), *([cv] * n_fetch))
    return out.reshape(bs, heads * hd)


def _ret_tables(pos):
    inv = 1.0 / (RET_ROPE_BASE ** jnp.linspace(0.0, 1.0, RET_DK // 2, dtype=F32))
    ang = pos.astype(F32)[:, None] * inv[None, :]
    return jnp.cos(ang), jnp.sin(ang)


def _ret_tables_interleaved(pos, heads):
    cos, sin = _ret_tables(pos)
    zero = jnp.zeros_like(sin)
    c = jnp.stack([cos, cos], axis=-1).reshape(pos.shape[0], RET_DK)
    a = jnp.stack([-sin, zero], axis=-1).reshape(pos.shape[0], RET_DK)
    b = jnp.stack([zero, sin], axis=-1).reshape(pos.shape[0], RET_DK)
    k_scale = RET_DK ** -0.5
    return tuple(jnp.concatenate([jnp.tile(tab, (1, heads)), jnp.tile(tab * k_scale, (1, heads))], axis=1)
                 for tab in (c, a, b))


def _diff_tables(pos):
    half = DIFF_ROT // 2
    inv = 1.0 / (ROPE_THETA ** (jnp.arange(half, dtype=F32) * 2.0 / DIFF_ROT))
    ang = pos.astype(F32)[:, None] * inv[None, :]
    cos, sin = jnp.cos(ang), jnp.sin(ang)
    n = pos.shape[0]
    ones = jnp.ones((n, DIFF_D - DIFF_ROT), F32)
    zeros_rest = jnp.zeros((n, DIFF_D - DIFF_ROT), F32)
    zeros_h = jnp.zeros((n, half), F32)
    c = jnp.concatenate([cos, cos, ones], axis=1)
    a = jnp.concatenate([-sin, zeros_h, zeros_rest], axis=1)
    b = jnp.concatenate([zeros_h, sin, zeros_rest], axis=1)
    rep = LANES // DIFF_D
    return tuple(jnp.tile(tab, (1, rep)) for tab in (c, a, b))


def _ret_prompt_weights(w_in, heads):
    hk = heads * RET_DK
    perm = np.concatenate([np.arange(0, RET_DK, 2), np.arange(1, RET_DK, 2)])
    qk_cols = np.concatenate([h * RET_DK + perm for h in range(2 * heads)])
    cols = np.concatenate([qk_cols, np.arange(2 * hk, w_in.shape[2])])
    return w_in[:, :, cols].astype(BF)


def kernel(x_prompt, x_sample, state_ret, cache_k, cache_v, page_table, ret_w_in, ret_gn_g, ret_w_o, diff_w_in, diff_lq1, diff_lk1, diff_lq2, diff_lk2, diff_subln_g, diff_w_o, ffn_w_in, ffn_w_down, ln1_g, ln1_b, ln2_g, ln2_b):
    bp, tp, d_model = x_prompt.shape
    bs, ts, _ = x_sample.shape
    assert ts == 1
    depth = ffn_w_in.shape[0]
    ret_heads = ret_w_o.shape[1] // RET_DV
    diff_heads = diff_w_o.shape[1] // (2 * DIFF_D)
    hd = diff_heads * 2 * DIFF_D
    past_len = page_table.shape[1] * PAGE_SIZE
    alpha = (2.0 * depth) ** 0.25
    mp = bp * tp

    pos_p = jnp.arange(tp)
    pos_s = jnp.broadcast_to(past_len + jnp.arange(ts), (bs,))
    ret_tab_p = _ret_tables(pos_p)
    ret_tab_s = _ret_tables_interleaved(pos_s, ret_heads)
    diff_tab_p = _diff_tables(pos_p)
    diff_tab_s = _diff_tables(pos_s)

    ret_chunk = RET_CHUNK if tp % RET_CHUNK == 0 else tp
    lg = jnp.log(1.0 - 2.0 ** (-5.0 - jnp.arange(ret_heads, dtype=F32)))
    lg_tab = jnp.broadcast_to(lg[:, None, None], (ret_heads, 1, max(ret_chunk, LANES)))

    tm_p = _row_tile(mp, PROJ_ROWS)
    tm_s = bs

    xp = x_prompt.reshape(mp, d_model)
    xs = x_sample.reshape(bs * ts, d_model)
    xp_b, xs_b = xp, xs

    n_diff = diff_w_in.shape[0]
    ret_p, kc_s, vc_s, ret_k_s, ret_v_s = [], [], [], [], []
    kc_p_all = vc_p_all = None

    hk, hv = ret_heads * RET_DK, ret_heads * RET_DV
    ret_w_p = _ret_prompt_weights(ret_w_in, ret_heads)
    ret_w_s = ret_w_in.astype(BF)
    ret_wo = ret_w_o.astype(BF)
    diff_w = diff_w_in.astype(BF)
    diff_wo = diff_w_o.astype(BF)
    ffn_wi = ffn_w_in.astype(BF)
    ffn_wd = ffn_w_down.astype(BF)

    for i in range(depth):
        j = i // N_MIXERS
        g1, b1 = ln1_g[i][None, :], ln1_b[i][None, :]
        g2, b2 = ln2_g[i][None, :], ln2_b[i][None, :]
        if i % N_MIXERS == 0:
            w_o = ret_wo
            gn = ret_gn_g[j][None, :]
            qk, v, g = _ret_proj(xp_b, ret_w_p, *ret_tab_p, j, heads=ret_heads, tm=tm_p, name=f"ret{j}_proj_p")
            mix_p, st_p = _ret_chunk(lg_tab, qk, v, g, gn, batch=bp, heads=ret_heads, chunk=ret_chunk,
                                     name=f"ret{j}_chunk_p")
            ret_p.append(st_p.reshape(bp, ret_heads, RET_DK, RET_DV))
            (qk_s,) = _shift_rope_proj(xs_b, ret_w_s, ret_tab_s, [F32], layer=j, col0=0, n=2 * hk, scale=1.0,
                                       shift=1, tm=tm_s, tn=PROJ_COLS, name=f"ret{j}_qk_s")
            (v_s,) = _proj_plain(xs_b, ret_w_s, [F32], layer=j, col0=2 * hk, n=hv, tm=tm_s, tn=PROJ_COLS,
                                 name=f"ret{j}_v_s")
            (g_s,) = _proj_plain(xs_b, ret_w_s, [F32], layer=j, col0=2 * hk + hv, n=hv, tm=tm_s, tn=PROJ_COLS,
                                 name=f"ret{j}_g_s")
            mix_s = _ret_step(lg_tab, qk_s, v_s, g_s, gn, state_ret, j, heads=ret_heads, name=f"ret{j}_step_s")
            ret_k_s.append(qk_s[:, hk:])
            ret_v_s.append(v_s)
        else:
            lam_init = 0.8 - 0.6 * math.exp(-0.3 * (i + 1))
            w_o = diff_wo
            lam_args = (diff_lq1[j][None, :], diff_lk1[j][None, :], diff_lq2[j][None, :], diff_lk2[j][None, :])
            sg = diff_subln_g[j][None, :]
            rot = DIFF_ROT // 2
            q, kc_p_all, k_b, vc_p_all, v_b = _diff_proj(xp_b, diff_w, diff_tab_p, kc_p_all, vc_p_all, j, n_diff,
                                                         q_scale=DIFF_D ** -0.5 * math.log2(math.e), shift=rot,
                                                         tm=tm_p, name=f"diff{j}_proj_p")
            tq = FLASH_TQ if tp % FLASH_TQ == 0 else tp
            tk = FLASH_TK if tp % FLASH_TK == 0 else tp
            mix_p = _diff_flash(q, k_b, v_b, *lam_args, sg, batch=bp, heads=diff_heads, tq=tq, tk=tk,
                                rb_full=min(FLASH_ROWS, tq), rb_diag=min(FLASH_ROWS_DIAG, tq),
                                lam_init=lam_init, name=f"diff{j}_flash_p")
            (q_s,) = _shift_rope_proj(xs_b, diff_w, diff_tab_s, [F32], layer=j, col0=0, n=hd, scale=DIFF_D ** -0.5,
                                      shift=rot, tm=tm_s, tn=PROJ_COLS, name=f"diff{j}_q_s")
            (kn,) = _shift_rope_proj(xs_b, diff_w, diff_tab_s, [F32], layer=j, col0=hd, n=hd, scale=1.0,
                                     shift=rot, tm=tm_s, tn=PROJ_COLS, name=f"diff{j}_k_s")
            (vn,) = _proj_plain(xs_b, diff_w, [F32], layer=j, col0=2 * hd, n=hd, tm=tm_s, tn=PROJ_COLS,
                                name=f"diff{j}_v_s")
            mix_s = _diff_decode(page_table, q_s, kn, vn, *lam_args, sg, cache_k, cache_v, j,
                                 heads=diff_heads, lam_init=lam_init, name=f"diff{j}_decode_s")
            kc_s.append(kn.reshape(bs, ts, diff_heads, 2 * DIFF_D))
            vc_s.append(vn.reshape(bs, ts, diff_heads, 2 * DIFF_D))

        xp, xp_b = _mix_ffn(mix_p, w_o, j, xp, g1, b1, ffn_wi, ffn_wd, i, g2, b2, alpha=alpha,
                            tm=_row_tile(mp, FFN_ROWS), th=FFN_COLS, name=f"l{i}_mix_ffn_p")
        xs, xs_b = _mix_ffn(mix_s, w_o, j, xs, g1, b1, ffn_wi, ffn_wd, i, g2, b2, alpha=alpha, tm=tm_s,
                            th=FFN_COLS, name=f"l{i}_mix_ffn_s")

    ret_s_all = _ret_new_state(lg_tab, jnp.stack(ret_k_s)[:, :, None, :], jnp.stack(ret_v_s)[:, :, None, :],
                               state_ret, heads=ret_heads, name="ret_new_state_s")
    return (xp.reshape(bp, tp, d_model), xs.reshape(bs, ts, d_model),
            jnp.stack(ret_p),
            kc_p_all.reshape(n_diff, bp, tp, diff_heads, 2 * DIFF_D),
            vc_p_all.reshape(n_diff, bp, tp, diff_heads, 2 * DIFF_D),
            ret_s_all, jnp.stack(kc_s), jnp.stack(vc_s))
```

```python
import functools
import math

import numpy as np
import jax
import jax.numpy as jnp
from jax import lax
from jax.experimental import pallas as pl
from jax.experimental.pallas import tpu as pltpu

BF = jnp.bfloat16
F32 = jnp.float32

RET_DK = 256
RET_DV = 512
RET_ROPE_BASE = 10000.0
DIFF_D = 64
DIFF_ROT = DIFF_D // 4
ROPE_THETA = 500000.0
PAGE_SIZE = 128
LN_EPS = 1e-5
N_MIXERS = 2

LANES = 128
SUBLANES = 8
PROJ_ROWS = 512
PROJ_COLS = 512
FFN_COLS = 256
NORM_ROWS = 256
FLASH_TQ = 2048
FLASH_TK = 2048
FLASH_ROWS = 256
FLASH_ROWS_DIAG = 512
RET_CHUNK = 512
FFN_ROWS = 512
DECODE_PAGES = 8
VMEM_LIMIT = 48 * 1024 * 1024


def _cparams(sem):
    return pltpu.CompilerParams(dimension_semantics=sem, vmem_limit_bytes=VMEM_LIMIT)


def _sigmoid(x):
    return 1.0 / (1.0 + jnp.exp(-x))


def _layer_norm(y, g, b):
    mu = jnp.mean(y, axis=-1, keepdims=True)
    d = y - mu
    var = jnp.mean(d * d, axis=-1, keepdims=True)
    return d * lax.rsqrt(var + LN_EPS) * g + b


def _row_tile(m, pref):
    return pref if m % pref == 0 else m


def _proj_plain_kernel(x_ref, w_ref, *o_refs):
    acc = jnp.dot(x_ref[...].astype(BF), w_ref[...], preferred_element_type=F32)
    for o_ref in o_refs:
        o_ref[...] = acc.astype(o_ref.dtype)


def _stacked_cols_spec(k, tn, layer, col0):
    assert col0 % tn == 0
    return pl.BlockSpec((None, k, tn), lambda i, j: (layer, 0, col0 // tn + j))


def _proj_plain(x, w, out_dtypes, *, layer, col0, n, tm, tn, name):
    m, k = x.shape
    return pl.pallas_call(
        _proj_plain_kernel,
        grid=(m // tm, n // tn),
        in_specs=[pl.BlockSpec((tm, k), lambda i, j: (i, 0)),
                  _stacked_cols_spec(k, tn, layer, col0)],
        out_specs=[pl.BlockSpec((tm, tn), lambda i, j: (i, j)) for _ in out_dtypes],
        out_shape=[jax.ShapeDtypeStruct((m, n), dt) for dt in out_dtypes],
        compiler_params=_cparams(("parallel", "parallel")),
        name=name,
    )(x, w)


def _shift_rope_kernel(x_ref, w_ref, c_ref, a_ref, b_ref, *o_refs, scale, shift):
    acc = jnp.dot(x_ref[...].astype(BF), w_ref[...], preferred_element_type=F32)
    tab_tiles = c_ref.shape[1] // LANES
    for t in range(acc.shape[1] // LANES):
        tt = t % tab_tiles
        cols = slice(tt * LANES, (tt + 1) * LANES)
        xs = acc[:, t * LANES:(t + 1) * LANES]
        up = pltpu.roll(xs, LANES - shift, axis=1)
        dn = pltpu.roll(xs, shift, axis=1)
        r = xs * c_ref[:, cols] + up * a_ref[:, cols] + dn * b_ref[:, cols]
        if scale != 1.0:
            r = r * scale
        for o_ref in o_refs:
            o_ref[:, t * LANES:(t + 1) * LANES] = r.astype(o_ref.dtype)


def _shift_rope_proj(x, w, tabs, out_dtypes, *, layer, col0, n, scale, shift, tm, tn, name):
    m, k = x.shape
    ntab = tabs[0].shape[0] // tm
    tw = min(tabs[0].shape[1], tn)
    ntab_cols = tabs[0].shape[1] // tw
    kern = functools.partial(_shift_rope_kernel, scale=scale, shift=shift)
    tab_spec = pl.BlockSpec((tm, tw), lambda i, j: (i % ntab, j % ntab_cols))
    return pl.pallas_call(
        kern,
        grid=(m // tm, n // tn),
        in_specs=[pl.BlockSpec((tm, k), lambda i, j: (i, 0)),
                  _stacked_cols_spec(k, tn, layer, col0),
                  tab_spec, tab_spec, tab_spec],
        out_specs=[pl.BlockSpec((tm, tn), lambda i, j: (i, j)) for _ in out_dtypes],
        out_shape=[jax.ShapeDtypeStruct((m, n), dt) for dt in out_dtypes],
        compiler_params=_cparams(("parallel", "parallel")),
        name=name,
    )(x, w, *tabs)


def _resident(block_shape, index_map):
    return pl.BlockSpec(block_shape, index_map, pipeline_mode=pl.Buffered(1))


def _ret_proj_kernel(x_ref, w_ref, cos_ref, sin_ref, qk_ref, v_ref, g_ref, *, heads, tn):
    xb = x_ref[...].astype(BF)
    hk, hv = heads * RET_DK, heads * RET_DV
    half = RET_DK // 2
    c = cos_ref[...]
    s = sin_ref[...]
    for h in range(2 * heads):
        acc = jnp.dot(xb, w_ref[:, h * RET_DK:(h + 1) * RET_DK], preferred_element_type=F32)
        x1 = acc[:, :half]
        x2 = acc[:, half:]
        r1 = x1 * c - x2 * s
        r2 = x2 * c + x1 * s
        if h >= heads:
            r1 = r1 * RET_DK ** -0.5
            r2 = r2 * RET_DK ** -0.5
        qk_ref[:, h * RET_DK:h * RET_DK + half] = r1.astype(qk_ref.dtype)
        qk_ref[:, h * RET_DK + half:(h + 1) * RET_DK] = r2.astype(qk_ref.dtype)
    for t in range(hv // tn):
        cols = slice(t * tn, (t + 1) * tn)
        v_ref[:, cols] = jnp.dot(xb, w_ref[:, 2 * hk + t * tn:2 * hk + (t + 1) * tn],
                                 preferred_element_type=F32).astype(v_ref.dtype)
    for t in range(hv // tn):
        cols = slice(t * tn, (t + 1) * tn)
        g_ref[:, cols] = jnp.dot(xb, w_ref[:, 2 * hk + hv + t * tn:2 * hk + hv + (t + 1) * tn],
                                 preferred_element_type=F32).astype(g_ref.dtype)


def _ret_proj(x, w_all, cos, sin, layer, *, heads, tm, name):
    m, k = x.shape
    hk, hv = heads * RET_DK, heads * RET_DV
    ntab = cos.shape[0] // tm
    kern = functools.partial(_ret_proj_kernel, heads=heads, tn=PROJ_COLS)
    tab_spec = pl.BlockSpec((tm, RET_DK // 2), lambda i: (i % ntab, 0))
    return pl.pallas_call(
        kern,
        grid=(m // tm,),
        in_specs=[pl.BlockSpec((tm, k), lambda i: (i, 0)),
                  _resident((None, k, 2 * hk + 2 * hv), lambda i: (layer, 0, 0)),
                  tab_spec, tab_spec],
        out_specs=[pl.BlockSpec((tm, 2 * hk), lambda i: (i, 0)),
                   pl.BlockSpec((tm, hv), lambda i: (i, 0)),
                   pl.BlockSpec((tm, hv), lambda i: (i, 0))],
        out_shape=[jax.ShapeDtypeStruct((m, 2 * hk), BF),
                   jax.ShapeDtypeStruct((m, hv), BF),
                   jax.ShapeDtypeStruct((m, hv), F32)],
        compiler_params=_cparams(("parallel",)),
        name=name,
    )(x, w_all, cos, sin)


def _diff_proj_kernel(*refs, hd, q_scale, shift, tn, layer, first):
    if first:
        x_ref, w_ref, c_ref, a_ref, b_ref, q_ref, kf_ref, kb_ref, vf_ref, vb_ref = refs
        for l in range(kf_ref.shape[0]):
            if l != layer:
                kf_ref[l] = jnp.zeros(kf_ref.shape[1:], F32)
                vf_ref[l] = jnp.zeros(vf_ref.shape[1:], F32)
        kf_dst, vf_dst = kf_ref.at[layer], vf_ref.at[layer]
    else:
        x_ref, w_ref, c_ref, a_ref, b_ref, _, _, q_ref, kf_dst, kb_ref, vf_dst, vb_ref = refs
    xb = x_ref[...].astype(BF)
    c = c_ref[...]
    a = a_ref[...]
    b = b_ref[...]

    def rope(acc, scale, col0, o_refs):
        for t in range(acc.shape[1] // LANES):
            xs = acc[:, t * LANES:(t + 1) * LANES]
            up = pltpu.roll(xs, LANES - shift, axis=1)
            dn = pltpu.roll(xs, shift, axis=1)
            r = xs * c + up * a + dn * b
            if scale != 1.0:
                r = r * scale
            for o_ref in o_refs:
                o_ref[:, col0 + t * LANES:col0 + (t + 1) * LANES] = r.astype(o_ref.dtype)

    for t in range(hd // tn):
        acc = jnp.dot(xb, w_ref[:, t * tn:(t + 1) * tn], preferred_element_type=F32)
        rope(acc, q_scale, t * tn, (q_ref,))
    for t in range(hd // tn):
        acc = jnp.dot(xb, w_ref[:, hd + t * tn:hd + (t + 1) * tn], preferred_element_type=F32)
        rope(acc, 1.0, t * tn, (kf_dst, kb_ref))
    for t in range(hd // tn):
        cols = slice(t * tn, (t + 1) * tn)
        acc = jnp.dot(xb, w_ref[:, 2 * hd + t * tn:2 * hd + (t + 1) * tn], preferred_element_type=F32)
        vf_dst[:, cols] = acc
        vb_ref[:, cols] = acc.astype(BF)


def _diff_proj(x, w_all, tabs, kf_all, vf_all, layer, n_layers, *, q_scale, shift, tm, name):
    m, k = x.shape
    hd = w_all.shape[2] // 3
    ntab = tabs[0].shape[0] // tm
    first = kf_all is None
    kern = functools.partial(_diff_proj_kernel, hd=hd, q_scale=q_scale, shift=shift, tn=PROJ_COLS, layer=layer,
                             first=first)
    tab_spec = pl.BlockSpec((tm, LANES), lambda i: (i % ntab, 0))
    row_spec = pl.BlockSpec((tm, hd), lambda i: (i, 0))
    in_specs = [pl.BlockSpec((tm, k), lambda i: (i, 0)),
                _resident((None, k, 3 * hd), lambda i: (layer, 0, 0)),
                tab_spec, tab_spec, tab_spec]
    args = [x, w_all, *tabs]
    if first:
        stack_spec = pl.BlockSpec((n_layers, tm, hd), lambda i: (0, i, 0))
        aliases = {}
    else:
        stack_spec = pl.BlockSpec((None, tm, hd), lambda i: (layer, i, 0))
        in_specs += [pl.BlockSpec(memory_space=pl.ANY), pl.BlockSpec(memory_space=pl.ANY)]
        args += [kf_all, vf_all]
        aliases = {5: 1, 6: 3}
    stacked = jax.ShapeDtypeStruct((n_layers, m, hd), F32)
    return pl.pallas_call(
        kern,
        grid=(m // tm,),
        in_specs=in_specs,
        out_specs=[row_spec, stack_spec, row_spec, stack_spec, row_spec],
        out_shape=[jax.ShapeDtypeStruct((m, hd), BF), stacked, jax.ShapeDtypeStruct((m, hd), BF), stacked,
                   jax.ShapeDtypeStruct((m, hd), BF)],
        input_output_aliases=aliases,
        compiler_params=_cparams(("parallel",)),
        name=name,
    )(*args)


def _mix_ffn_kernel(a_ref, wo_ref, x_ref, g1_ref, b1_ref, wg_ref, wu_ref, wd_ref, g2_ref, b2_ref,
                    o_ref, ob_ref, *, alpha, th, rb):
    wo = wo_ref[...]
    x1_rows = []
    for r in range(a_ref.shape[0] // rb):
        rows = pl.ds(r * rb, rb)
        sub = jnp.dot(a_ref[rows, :].astype(BF), wo, preferred_element_type=F32)
        x1_rows.append(_layer_norm(alpha * x_ref[rows, :] + sub, g1_ref[...], b1_ref[...]))
    x1 = jnp.concatenate(x1_rows, axis=0)
    xb = x1.astype(BF)
    hidden = wd_ref.shape[0]
    hs = []
    for c in range(hidden // th):
        cols = slice(c * th, (c + 1) * th)
        gate = jnp.dot(xb, wg_ref[:, cols], preferred_element_type=F32)
        up = jnp.dot(xb, wu_ref[:, cols], preferred_element_type=F32)
        hs.append((gate * _sigmoid(gate) * up).astype(BF))
    h = jnp.concatenate(hs, axis=1)
    sub = jnp.dot(h, wd_ref[...], preferred_element_type=F32)
    y = _layer_norm(alpha * x1 + sub, g2_ref[...], b2_ref[...])
    o_ref[...] = y
    ob_ref[...] = y.astype(BF)


def _mix_ffn(a, w_o, mixer, x, g1, b1, w_in, w_down, layer, g2, b2, *, alpha, tm, th, name):
    m, ka = a.shape
    d = x.shape[1]
    hidden = w_down.shape[1]
    kern = functools.partial(_mix_ffn_kernel, alpha=alpha, th=th, rb=min(tm, NORM_ROWS))
    vec = pl.BlockSpec((1, d), lambda i: (0, 0))
    return pl.pallas_call(
        kern,
        grid=(m // tm,),
        in_specs=[pl.BlockSpec((tm, ka), lambda i: (i, 0)),
                  _resident((None, ka, d), lambda i: (mixer, 0, 0)),
                  pl.BlockSpec((tm, d), lambda i: (i, 0)),
                  vec, vec,
                  _resident((None, d, hidden), lambda i: (layer, 0, 0)),
                  _resident((None, d, hidden), lambda i: (layer, 0, 1)),
                  _resident((None, hidden, d), lambda i: (layer, 0, 0)),
                  vec, vec],
        out_specs=[pl.BlockSpec((tm, d), lambda i: (i, 0)),
                   pl.BlockSpec((tm, d), lambda i: (i, 0))],
        out_shape=[jax.ShapeDtypeStruct((m, d), F32), jax.ShapeDtypeStruct((m, d), BF)],
        compiler_params=_cparams(("parallel",)),
        name=name,
    )(a, w_o, x, g1, b1, w_in, w_in, w_down, g2, b2)


def _ret_chunk_kernel(lg_ref, q_ref, k_ref, v_ref, g_ref, gn_ref, o_ref, st_ref, s_ref, decay_ref, *, chunk):
    c = pl.program_id(2)
    half = RET_DK // 2
    lg_row = lg_ref[0]
    lg = lg_row[:, :1]

    @pl.when(c == 0)
    def _():
        s_ref[...] = jnp.zeros_like(s_ref)
        ii = lax.broadcasted_iota(jnp.int32, (chunk, chunk), 0)
        jj = lax.broadcasted_iota(jnp.int32, (chunk, chunk), 1)
        rel = (ii - jj).astype(F32)
        decay_ref[...] = jnp.where(rel >= 0, jnp.exp(lg_row * jnp.maximum(rel, 0.0)), 0.0)

    idx = lax.broadcasted_iota(jnp.int32, (chunk, 1), 0).astype(F32)
    q_decay = jnp.exp(lg * (idx + 1.0))
    k_decay = jnp.exp(lg * (chunk - 1.0 - idx))
    chunk_decay = jnp.exp(lg * float(chunk))

    q = q_ref[...]
    k = k_ref[...]
    v = v_ref[...]
    s_old = s_ref[...]
    scores = lax.dot_general(q, k, (((1,), (1,)), ((), ())), preferred_element_type=F32) * decay_ref[...]
    inner = jnp.dot(scores.astype(BF), v, preferred_element_type=F32)
    cross = jnp.dot(q, s_old.astype(BF), preferred_element_type=F32) * q_decay
    kd_t = (k.astype(F32) * k_decay).T.astype(BF)
    s_ref[...] = chunk_decay * s_old + jnp.dot(kd_t, v, preferred_element_type=F32)

    o = inner + cross
    mu = jnp.mean(o, axis=-1, keepdims=True)
    d = o - mu
    var = jnp.mean(d * d, axis=-1, keepdims=True)
    on = d * lax.rsqrt(var + LN_EPS) * gn_ref[...]
    gt = g_ref[...].astype(F32)
    o_ref[...] = (gt * _sigmoid(gt) * on).astype(o_ref.dtype)

    @pl.when(c == pl.num_programs(2) - 1)
    def _():
        st_ref[0, 0, :, :RET_DV] = s_ref[:half, :]
        st_ref[0, 0, :, RET_DV:] = s_ref[half:, :]


def _ret_chunk(lg_tab, qk, v, g, gn_g, *, batch, heads, chunk, name):
    m = qk.shape[0]
    t = m // batch
    nc = t // chunk
    kern = functools.partial(_ret_chunk_kernel, chunk=chunk)
    return pl.pallas_call(
        kern,
        grid=(batch, heads, nc),
        in_specs=[pl.BlockSpec((1, 1, chunk), lambda b, h, c: (h, 0, 0)),
                  pl.BlockSpec((chunk, RET_DK), lambda b, h, c: (b * nc + c, h)),
                  pl.BlockSpec((chunk, RET_DK), lambda b, h, c: (b * nc + c, heads + h)),
                  pl.BlockSpec((chunk, RET_DV), lambda b, h, c: (b * nc + c, h)),
                  pl.BlockSpec((chunk, RET_DV), lambda b, h, c: (b * nc + c, h)),
                  pl.BlockSpec((1, RET_DV), lambda b, h, c: (0, h))],
        out_specs=[pl.BlockSpec((chunk, RET_DV), lambda b, h, c: (b * nc + c, h)),
                   pl.BlockSpec((1, 1, RET_DK // 2, 2 * RET_DV), lambda b, h, c: (b, h, 0, 0))],
        out_shape=[jax.ShapeDtypeStruct((m, heads * RET_DV), BF),
                   jax.ShapeDtypeStruct((batch, heads, RET_DK // 2, 2 * RET_DV), F32)],
        scratch_shapes=[pltpu.VMEM((RET_DK, RET_DV), F32), pltpu.VMEM((chunk, chunk), F32)],
        compiler_params=_cparams(("parallel", "parallel", "arbitrary")),
        name=name,
    )(lg_tab, qk, qk, v, g, gn_g)


def _ret_step_kernel(lg_ref, q_ref, k_ref, v_ref, g_ref, gn_ref, s0_ref, o_ref, *, heads):
    qrow = q_ref[0]
    krow = k_ref[0]
    vrow = v_ref[0]
    grow = g_ref[0]
    for h in range(heads):
        gamma = jnp.exp(lg_ref[h][:, :1])
        qh = qrow[:, h * RET_DK:(h + 1) * RET_DK]
        kh = krow[:, h * RET_DK:(h + 1) * RET_DK]
        vh = vrow[:, h * RET_DV:(h + 1) * RET_DV]
        s0 = s0_ref[0, 0, h]
        q8 = jnp.broadcast_to(qh, (SUBLANES, RET_DK)).astype(BF)
        cross = jnp.dot(q8, s0.astype(BF), preferred_element_type=F32)[:1]
        qk = jnp.sum(qh * kh, axis=-1, keepdims=True)
        o = qk * vh + gamma * cross
        mu = jnp.mean(o, axis=-1, keepdims=True)
        d = o - mu
        var = jnp.mean(d * d, axis=-1, keepdims=True)
        on = d * lax.rsqrt(var + LN_EPS) * gn_ref[:, h * RET_DV:(h + 1) * RET_DV]
        gt = grow[:, h * RET_DV:(h + 1) * RET_DV]
        o_ref[0, :, h * RET_DV:(h + 1) * RET_DV] = (gt * _sigmoid(gt) * on).astype(o_ref.dtype)


def _ret_step(lg_tab, qk, v, g, gn_g, state_all, layer, *, heads, name):
    bs = qk.shape[0]
    hk = heads * RET_DK
    hv = heads * RET_DV
    q3 = qk.reshape(bs, 1, 2 * hk)
    v3 = v.reshape(bs, 1, hv)
    g3 = g.reshape(bs, 1, hv)
    kern = functools.partial(_ret_step_kernel, heads=heads)
    out = pl.pallas_call(
        kern,
        grid=(bs,),
        in_specs=[pl.BlockSpec(lg_tab.shape, lambda b: (0, 0, 0)),
                  pl.BlockSpec((1, 1, hk), lambda b: (b, 0, 0)),
                  pl.BlockSpec((1, 1, hk), lambda b: (b, 0, 1)),
                  pl.BlockSpec((1, 1, hv), lambda b: (b, 0, 0)),
                  pl.BlockSpec((1, 1, hv), lambda b: (b, 0, 0)),
                  pl.BlockSpec((1, hv), lambda b: (0, 0)),
                  pl.BlockSpec((1, 1, heads, RET_DK, RET_DV), lambda b: (layer, b, 0, 0, 0))],
        out_specs=pl.BlockSpec((1, 1, hv), lambda b: (b, 0, 0)),
        out_shape=jax.ShapeDtypeStruct((bs, 1, hv), BF),
        compiler_params=_cparams(("parallel",)),
        name=name,
    )(lg_tab, q3, q3, v3, g3, gn_g, state_all)
    return out.reshape(bs, hv)


def _ret_new_state_kernel(lg_ref, k_ref, v_ref, s0_ref, st_ref, *, heads):
    krow = k_ref[0, 0]
    vrow = v_ref[0, 0]
    for h in range(heads):
        gamma = jnp.exp(lg_ref[h][:, :1])
        kh = krow[:, h * RET_DK:(h + 1) * RET_DK]
        vh = vrow[:, h * RET_DV:(h + 1) * RET_DV]
        k_col = jnp.broadcast_to(kh, (SUBLANES, RET_DK)).T[:, :1]
        st_ref[0, 0, h] = gamma * s0_ref[0, 0, h] + k_col * vh


def _ret_new_state(lg_tab, k_all, v_all, state_all, *, heads, name):
    n_layers, bs = state_all.shape[:2]
    hk = heads * RET_DK
    hv = heads * RET_DV
    kern = functools.partial(_ret_new_state_kernel, heads=heads)
    st_spec = pl.BlockSpec((1, 1, heads, RET_DK, RET_DV), lambda l, b: (l, b, 0, 0, 0))
    return pl.pallas_call(
        kern,
        grid=(n_layers, bs),
        in_specs=[pl.BlockSpec(lg_tab.shape, lambda l, b: (0, 0, 0)),
                  pl.BlockSpec((1, 1, 1, hk), lambda l, b: (l, b, 0, 0)),
                  pl.BlockSpec((1, 1, 1, hv), lambda l, b: (l, b, 0, 0)),
                  st_spec],
        out_specs=st_spec,
        out_shape=jax.ShapeDtypeStruct(state_all.shape, F32),
        compiler_params=_cparams(("parallel", "parallel")),
        name=name,
    )(lg_tab, k_all, v_all, state_all)


def _diff_lambda(lq1_ref, lk1_ref, lq2_ref, lk2_ref, lam_init):
    a = jnp.sum(lq1_ref[...] * lk1_ref[...], axis=-1, keepdims=True)
    b = jnp.sum(lq2_ref[...] * lk2_ref[...], axis=-1, keepdims=True)
    return jnp.exp(a) - jnp.exp(b) + lam_init


def _diff_flash_kernel(qi_ref, kj_ref, q_ref, k_ref, v_ref, lq1_ref, lk1_ref, lq2_ref, lk2_ref, sg_ref,
                       o_ref, q1_ref, q2_ref, m1_ref, m2_ref, l1_ref, l2_ref, a1_ref, a2_ref,
                       *, tq, tk, rb_full, rb_diag, lam_init):
    step = pl.program_id(2)
    qi = qi_ref[step]
    kj = kj_ref[step]
    last_kj = ((qi + 1) * tq - 1) // tk
    comps = ((q1_ref, m1_ref, l1_ref, a1_ref), (q2_ref, m2_ref, l2_ref, a2_ref))

    @pl.when(kj == 0)
    def _():
        q = q_ref[...]
        lane = lax.broadcasted_iota(jnp.int32, q.shape, 1)
        zero = jnp.zeros_like(q)
        q1_ref[...] = jnp.where(lane < DIFF_D, q, zero)
        q2_ref[...] = jnp.where(lane >= DIFF_D, q, zero)
        for _, m_ref, l_ref, a_ref in comps:
            m_ref[...] = jnp.full_like(m_ref, -jnp.inf)
            l_ref[...] = jnp.zeros_like(l_ref)
            a_ref[...] = jnp.zeros_like(a_ref)

    def update(masked):
        rb = rb_diag if masked else rb_full
        k = k_ref[...]
        v = v_ref[...]
        v_ext = jnp.concatenate([v, jnp.ones_like(v)], axis=1)
        tri = (lax.broadcasted_iota(jnp.int32, (rb, rb), 1) <= lax.broadcasted_iota(jnp.int32, (rb, rb), 0))
        def chain(qc_ref, m_ref, l_ref, a_ref, r):
            rows = pl.ds(r * rb, rb)
            q_r = qc_ref[rows, :]
            nt = (((1,), (1,)), ((), ()))
            if masked and tq == tk:
                lo = r * rb
                parts = []
                if lo > 0:
                    parts.append((lax.dot_general(q_r, k[:lo], nt, preferred_element_type=F32), v_ext[:lo]))
                s_d = lax.dot_general(q_r, k[lo:lo + rb], nt, preferred_element_type=F32)
                parts.append((jnp.where(tri, s_d, -jnp.inf), v_ext[lo:lo + rb]))
            else:
                s = lax.dot_general(q_r, k, nt, preferred_element_type=F32)
                if masked:
                    row = lax.broadcasted_iota(jnp.int32, (rb, tk), 0) + (qi * tq + r * rb)
                    col = lax.broadcasted_iota(jnp.int32, (rb, tk), 1) + kj * tk
                    s = jnp.where(col <= row, s, -jnp.inf)
                parts = [(s, v_ext)]
            m_prev = m_ref[rows, :]
            m_new = m_prev
            for s, _ in parts:
                m_new = jnp.maximum(m_new, jnp.max(s, axis=-1, keepdims=True))
            alpha = jnp.exp2(m_prev - m_new)
            pv = None
            for s, vx in parts:
                p = jnp.exp2(s - jnp.tile(m_new, (1, s.shape[1] // LANES)))
                d = jnp.dot(p.astype(BF), vx, preferred_element_type=F32)
                pv = d if pv is None else pv + d
            a_ref[rows, :] = alpha * a_ref[rows, :] + pv[:, :LANES]
            l_ref[rows, :] = alpha * l_ref[rows, :] + pv[:, LANES:]
            m_ref[rows, :] = m_new

        for qc_ref, m_ref, l_ref, a_ref in comps:
            for r in range(tq // rb):
                chain(qc_ref, m_ref, l_ref, a_ref, r)


    straddles = (kj + 1) * tk - 1 > qi * tq

    @pl.when(straddles)
    def _():
        update(True)

    @pl.when(jnp.logical_not(straddles))
    def _():
        update(False)

    @pl.when(kj == last_kj)
    def _():
        lam = _diff_lambda(lq1_ref, lk1_ref, lq2_ref, lk2_ref, lam_init)
        o = a1_ref[...] / l1_ref[...] - lam * (a2_ref[...] / l2_ref[...])
        o = o * lax.rsqrt(jnp.mean(o * o, axis=-1, keepdims=True) + LN_EPS)
        o_ref[...] = (o * sg_ref[...] * (1.0 - lam_init)).astype(o_ref.dtype)


def _diff_flash(q, k, v, lq1, lk1, lq2, lk2, subln_g, *, batch, heads, tq, tk, rb_full, rb_diag, lam_init, name):
    m = q.shape[0]
    t = m // batch
    nq, nk = t // tq, t // tk
    qi_list, kj_list = [], []
    for i in range(nq):
        for j in range(((i + 1) * tq - 1) // tk + 1):
            qi_list.append(i)
            kj_list.append(j)
    qi_tab = jnp.asarray(np.array(qi_list, np.int32))
    kj_tab = jnp.asarray(np.array(kj_list, np.int32))
    hd = 2 * DIFF_D
    assert hd == LANES
    kern = functools.partial(_diff_flash_kernel, tq=tq, tk=tk, rb_full=rb_full, rb_diag=rb_diag, lam_init=lam_init)
    vec_spec = pl.BlockSpec((1, DIFF_D), lambda b, h, s, qi, kj: (0, 0))
    grid_spec = pltpu.PrefetchScalarGridSpec(
        num_scalar_prefetch=2,
        grid=(batch, heads, len(qi_list)),
        in_specs=[pl.BlockSpec((tq, hd), lambda b, h, s, qi, kj: (b * nq + qi[s], h)),
                  pl.BlockSpec((tk, hd), lambda b, h, s, qi, kj: (b * nk + kj[s], h)),
                  pl.BlockSpec((tk, hd), lambda b, h, s, qi, kj: (b * nk + kj[s], h)),
                  vec_spec, vec_spec, vec_spec, vec_spec,
                  pl.BlockSpec((1, hd), lambda b, h, s, qi, kj: (0, h))],
        out_specs=pl.BlockSpec((tq, hd), lambda b, h, s, qi, kj: (b * nq + qi[s], h)),
        scratch_shapes=[pltpu.VMEM((tq, hd), BF), pltpu.VMEM((tq, hd), BF)]
                       + [pltpu.VMEM((tq, hd), F32) for _ in range(6)],
    )
    return pl.pallas_call(
        kern,
        grid_spec=grid_spec,
        out_shape=jax.ShapeDtypeStruct((m, heads * hd), BF),
        compiler_params=_cparams(("parallel", "parallel", "arbitrary")),
        name=name,
    )(qi_tab, kj_tab, q, k, v, lq1, lk1, lq2, lk2, subln_g)


def _diff_decode_kernel(pt_ref, q_ref, kn_ref, vn_ref, lq1_ref, lk1_ref, lq2_ref, lk2_ref, sg_ref, *rest,
                        heads, n_fetch, lam_init):
    del pt_ref
    k_refs = rest[:n_fetch]
    v_refs = rest[n_fetch:2 * n_fetch]
    o_ref, m_ref, l_ref, acc_ref = rest[2 * n_fetch:]
    p_idx = pl.program_id(1)
    rows = 2 * heads

    @pl.when(p_idx == 0)
    def _():
        m_ref[...] = jnp.full_like(m_ref, -jnp.inf)
        l_ref[...] = jnp.zeros_like(l_ref)
        acc_ref[...] = jnp.zeros_like(acc_ref)

    q8 = q_ref[0]
    lane = lax.broadcasted_iota(jnp.int32, q8.shape, 1)
    q16 = jnp.concatenate([jnp.where(lane < DIFF_D, q8, 0.0),
                           jnp.where(lane >= DIFF_D, q8, 0.0)], axis=0)
    q16b = q16.astype(BF)

    ncol = k_refs[0].shape[1]
    r_head = lax.broadcasted_iota(jnp.int32, (rows, ncol), 0) % heads
    c_head = lax.broadcasted_iota(jnp.int32, (rows, ncol), 1) % heads
    same_head = r_head == c_head
    s_pages = []
    for k_ref in k_refs:
        s = lax.dot_general(q16b, k_ref[0].astype(BF), (((1,), (1,)), ((), ())),
                            preferred_element_type=F32)
        s_pages.append(jnp.where(same_head, s, -jnp.inf))
    m_prev = m_ref[...]
    m_new = m_prev
    for s in s_pages:
        m_new = jnp.maximum(m_new, jnp.max(s, axis=-1, keepdims=True))
    alpha = jnp.exp(m_prev - m_new)
    l_new = alpha * l_ref[...]
    acc = alpha * acc_ref[...]
    for s, v_ref in zip(s_pages, v_refs):
        p = jnp.exp(s - m_new)
        l_new = l_new + jnp.sum(p, axis=-1, keepdims=True)
        acc = acc + jnp.dot(p.astype(BF), v_ref[0].astype(BF), preferred_element_type=F32)
    l_ref[...] = l_new
    acc_ref[...] = acc
    m_ref[...] = m_new

    @pl.when(p_idx == pl.num_programs(1) - 1)
    def _():
        kn = kn_ref[0]
        vn = vn_ref[0]
        kn2 = jnp.concatenate([kn, kn], axis=0)
        vn2 = jnp.concatenate([vn, vn], axis=0)
        s_n = jnp.sum(q16 * kn2, axis=-1, keepdims=True)
        m_fin = jnp.maximum(m_new, s_n)
        beta = jnp.exp(m_new - m_fin)
        p_n = jnp.exp(s_n - m_fin)
        l_fin = beta * l_new + p_n
        acc_fin = beta * acc + p_n * vn2
        lam = _diff_lambda(lq1_ref, lk1_ref, lq2_ref, lk2_ref, lam_init)
        o = acc_fin[:heads] / l_fin[:heads] - lam * (acc_fin[heads:] / l_fin[heads:])
        o = o * lax.rsqrt(jnp.mean(o * o, axis=-1, keepdims=True) + LN_EPS)
        o_ref[0] = (o * sg_ref[...] * (1.0 - lam_init)).astype(o_ref.dtype)


def _diff_decode(page_table, q, kn, vn, lq1, lk1, lq2, lk2, subln_g, cache_k, cache_v, layer, *,
                 heads, lam_init, name):
    bs, n_pages = page_table.shape
    hd = 2 * DIFF_D
    n_layers, n_phys = cache_k.shape[:2]
    ck = cache_k.reshape(n_layers * n_phys, PAGE_SIZE * heads, hd)
    cv = cache_v.reshape(n_layers * n_phys, PAGE_SIZE * heads, hd)
    q3 = q.reshape(bs, heads, hd)
    kn3 = kn.reshape(bs, heads, hd)
    vn3 = vn.reshape(bs, heads, hd)
    sg = subln_g.reshape(heads, hd)
    pt = page_table.reshape(-1)
    n_fetch = DECODE_PAGES if n_pages % DECODE_PAGES == 0 else 1
    base = layer * n_phys
    kern = functools.partial(_diff_decode_kernel, heads=heads, n_fetch=n_fetch, lam_init=lam_init)
    row_spec = pl.BlockSpec((1, heads, hd), lambda b, p, pt: (b, 0, 0))
    vec_spec = pl.BlockSpec((1, DIFF_D), lambda b, p, pt: (0, 0))

    def page_spec(r):
        return pl.BlockSpec((1, PAGE_SIZE * heads, hd),
                            lambda b, p, pt: (base + pt[b * n_pages + p * n_fetch + r], 0, 0))

    page_specs = [page_spec(r) for r in range(n_fetch)]
    grid_spec = pltpu.PrefetchScalarGridSpec(
        num_scalar_prefetch=1,
        grid=(bs, n_pages // n_fetch),
        in_specs=[row_spec, row_spec, row_spec, vec_spec, vec_spec, vec_spec, vec_spec,
                  pl.BlockSpec((heads, hd), lambda b, p, pt: (0, 0))] + page_specs + page_specs,
        out_specs=pl.BlockSpec((1, heads, hd), lambda b, p, pt: (b, 0, 0)),
        scratch_shapes=[pltpu.VMEM((2 * heads, 1), F32),
                        pltpu.VMEM((2 * heads, 1), F32),
                        pltpu.VMEM((2 * heads, hd), F32)],
    )
    out = pl.pallas_call(
        kern,
        grid_spec=grid_spec,
        out_shape=jax.ShapeDtypeStruct((bs, heads, hd), BF),
        compiler_params=_cparams(("parallel", "arbitrary")),
        name=name,
    )(pt, q3, kn3, vn3, lq1, lk1, lq2, lk2, sg, *([ck] * n_fetch), *([cv] * n_fetch))
    return out.reshape(bs, heads * hd)


def _ret_tables(pos):
    inv = 1.0 / (RET_ROPE_BASE ** jnp.linspace(0.0, 1.0, RET_DK // 2, dtype=F32))
    ang = pos.astype(F32)[:, None] * inv[None, :]
    return jnp.cos(ang), jnp.sin(ang)


def _ret_tables_interleaved(pos, heads):
    cos, sin = _ret_tables(pos)
    zero = jnp.zeros_like(sin)
    c = jnp.stack([cos, cos], axis=-1).reshape(pos.shape[0], RET_DK)
    a = jnp.stack([-sin, zero], axis=-1).reshape(pos.shape[0], RET_DK)
    b = jnp.stack([zero, sin], axis=-1).reshape(pos.shape[0], RET_DK)
    k_scale = RET_DK ** -0.5
    return tuple(jnp.concatenate([jnp.tile(tab, (1, heads)), jnp.tile(tab * k_scale, (1, heads))], axis=1)
                 for tab in (c, a, b))


def _diff_tables(pos):
    half = DIFF_ROT // 2
    inv = 1.0 / (ROPE_THETA ** (jnp.arange(half, dtype=F32) * 2.0 / DIFF_ROT))
    ang = pos.astype(F32)[:, None] * inv[None, :]
    cos, sin = jnp.cos(ang), jnp.sin(ang)
    n = pos.shape[0]
    ones = jnp.ones((n, DIFF_D - DIFF_ROT), F32)
    zeros_rest = jnp.zeros((n, DIFF_D - DIFF_ROT), F32)
    zeros_h = jnp.zeros((n, half), F32)
    c = jnp.concatenate([cos, cos, ones], axis=1)
    a = jnp.concatenate([-sin, zeros_h, zeros_rest], axis=1)
    b = jnp.concatenate([zeros_h, sin, zeros_rest], axis=1)
    rep = LANES // DIFF_D
    return tuple(jnp.tile(tab, (1, rep)) for tab in (c, a, b))


def _ret_prompt_weights(w_in, heads):
    hk = heads * RET_DK
    perm = np.concatenate([np.arange(0, RET_DK, 2), np.arange(1, RET_DK, 2)])
    qk_cols = np.concatenate([h * RET_DK + perm for h in range(2 * heads)])
    w_qk = w_in[:, :2 * hk][:, qk_cols]
    return jnp.concatenate([w_qk, w_in[:, 2 * hk:]], axis=1).astype(BF)[None]


def kernel(x_prompt, x_sample, state_ret, cache_k, cache_v, page_table, ret_w_in, ret_gn_g, ret_w_o, diff_w_in, diff_lq1, diff_lk1, diff_lq2, diff_lk2, diff_subln_g, diff_w_o, ffn_w_in, ffn_w_down, ln1_g, ln1_b, ln2_g, ln2_b):
    bp, tp, d_model = x_prompt.shape
    bs, ts, _ = x_sample.shape
    assert ts == 1
    depth = ffn_w_in.shape[0]
    ret_heads = ret_w_o.shape[1] // RET_DV
    diff_heads = diff_w_o.shape[1] // (2 * DIFF_D)
    hd = diff_heads * 2 * DIFF_D
    past_len = page_table.shape[1] * PAGE_SIZE
    alpha = (2.0 * depth) ** 0.25
    mp = bp * tp

    pos_p = jnp.arange(tp)
    pos_s = jnp.broadcast_to(past_len + jnp.arange(ts), (bs,))
    ret_tab_p = _ret_tables(pos_p)
    ret_tab_s = _ret_tables_interleaved(pos_s, ret_heads)
    diff_tab_p = _diff_tables(pos_p)
    diff_tab_s = _diff_tables(pos_s)

    ret_chunk = RET_CHUNK if tp % RET_CHUNK == 0 else tp
    lg = jnp.log(1.0 - 2.0 ** (-5.0 - jnp.arange(ret_heads, dtype=F32)))
    lg_tab = jnp.broadcast_to(lg[:, None, None], (ret_heads, 1, max(ret_chunk, LANES)))

    tm_p = _row_tile(mp, PROJ_ROWS)
    tm_s = bs

    xp = x_prompt.reshape(mp, d_model)
    xs = x_sample.reshape(bs * ts, d_model)
    xp_b, xs_b = xp, xs

    n_diff = diff_w_in.shape[0]
    ret_p, kc_s, vc_s, ret_k_s, ret_v_s = [], [], [], [], []
    kc_p_all = vc_p_all = None

    hk, hv = ret_heads * RET_DK, ret_heads * RET_DV
    ret_w_s = ret_w_in[:, :, :2 * hk].astype(BF)
    ret_wo = ret_w_o.astype(BF)
    diff_w = diff_w_in.astype(BF)
    diff_wo = diff_w_o.astype(BF)
    ffn_wi = ffn_w_in.astype(BF)
    ffn_wd = ffn_w_down.astype(BF)

    for i in range(depth):
        j = i // N_MIXERS
        g1, b1 = ln1_g[i][None, :], ln1_b[i][None, :]
        g2, b2 = ln2_g[i][None, :], ln2_b[i][None, :]
        if i % N_MIXERS == 0:
            w_o = ret_wo
            gn = ret_gn_g[j][None, :]
            ret_w_p = _ret_prompt_weights(ret_w_in[j], ret_heads)
            qk, v, g = _ret_proj(xp_b, ret_w_p, *ret_tab_p, 0, heads=ret_heads, tm=tm_p, name=f"ret{j}_proj_p")
            mix_p, st_p = _ret_chunk(lg_tab, qk, v, g, gn, batch=bp, heads=ret_heads, chunk=ret_chunk,
                                     name=f"ret{j}_chunk_p")
            ret_p.append(st_p.reshape(bp, ret_heads, RET_DK, RET_DV))
            (qk_s,) = _shift_rope_proj(xs_b, ret_w_s, ret_tab_s, [F32], layer=j, col0=0, n=2 * hk, scale=1.0,
                                       shift=1, tm=tm_s, tn=PROJ_COLS, name=f"ret{j}_qk_s")
            (v_s,) = _proj_plain(xs_b, ret_w_p, [F32], layer=0, col0=2 * hk, n=hv, tm=tm_s, tn=PROJ_COLS,
                                 name=f"ret{j}_v_s")
            (g_s,) = _proj_plain(xs_b, ret_w_p, [F32], layer=0, col0=2 * hk + hv, n=hv, tm=tm_s, tn=PROJ_COLS,
                                 name=f"ret{j}_g_s")
            mix_s = _ret_step(lg_tab, qk_s, v_s, g_s, gn, state_ret, j, heads=ret_heads, name=f"ret{j}_step_s")
            ret_k_s.append(qk_s[:, hk:])
            ret_v_s.append(v_s)
        else:
            lam_init = 0.8 - 0.6 * math.exp(-0.3 * (i + 1))
            w_o = diff_wo
            lam_args = (diff_lq1[j][None, :], diff_lk1[j][None, :], diff_lq2[j][None, :], diff_lk2[j][None, :])
            sg = diff_subln_g[j][None, :]
            rot = DIFF_ROT // 2
            q, kc_p_all, k_b, vc_p_all, v_b = _diff_proj(xp_b, diff_w, diff_tab_p, kc_p_all, vc_p_all, j, n_diff,
                                                         q_scale=DIFF_D ** -0.5 * math.log2(math.e), shift=rot,
                                                         tm=tm_p, name=f"diff{j}_proj_p")
            tq = FLASH_TQ if tp % FLASH_TQ == 0 else tp
            tk = FLASH_TK if tp % FLASH_TK == 0 else tp
            mix_p = _diff_flash(q, k_b, v_b, *lam_args, sg, batch=bp, heads=diff_heads, tq=tq, tk=tk,
                                rb_full=min(FLASH_ROWS, tq), rb_diag=min(FLASH_ROWS_DIAG, tq),
                                lam_init=lam_init, name=f"diff{j}_flash_p")
            (q_s,) = _shift_rope_proj(xs_b, diff_w, diff_tab_s, [F32], layer=j, col0=0, n=hd, scale=DIFF_D ** -0.5,
                                      shift=rot, tm=tm_s, tn=PROJ_COLS, name=f"diff{j}_q_s")
            (kn,) = _shift_rope_proj(xs_b, diff_w, diff_tab_s, [F32], layer=j, col0=hd, n=hd, scale=1.0,
                                     shift=rot, tm=tm_s, tn=PROJ_COLS, name=f"diff{j}_k_s")
            (vn,) = _proj_plain(xs_b, diff_w, [F32], layer=j, col0=2 * hd, n=hd, tm=tm_s, tn=PROJ_COLS,
                                name=f"diff{j}_v_s")
            mix_s = _diff_decode(page_table, q_s, kn, vn, *lam_args, sg, cache_k, cache_v, j,
                                 heads=diff_heads, lam_init=lam_init, name=f"diff{j}_decode_s")
            kc_s.append(kn.reshape(bs, ts, diff_heads, 2 * DIFF_D))
            vc_s.append(vn.reshape(bs, ts, diff_heads, 2 * DIFF_D))

        xp, xp_b = _mix_ffn(mix_p, w_o, j, xp, g1, b1, ffn_wi, ffn_wd, i, g2, b2, alpha=alpha,
                            tm=_row_tile(mp, FFN_ROWS), th=FFN_COLS, name=f"l{i}_mix_ffn_p")
        xs, xs_b = _mix_ffn(mix_s, w_o, j, xs, g1, b1, ffn_wi, ffn_wd, i, g2, b2, alpha=alpha, tm=tm_s,
                            th=FFN_COLS, name=f"l{i}_mix_ffn_s")

    ret_s_all = _ret_new_state(lg_tab, jnp.stack(ret_k_s)[:, :, None, :], jnp.stack(ret_v_s)[:, :, None, :],
                               state_ret, heads=ret_heads, name="ret_new_state_s")
    return (xp.reshape(bp, tp, d_model), xs.reshape(bs, ts, d_model),
            jnp.stack(ret_p),
            kc_p_all.reshape(n_diff, bp, tp, diff_heads, 2 * DIFF_D),
            vc_p_all.reshape(n_diff, bp, tp, diff_heads, 2 * DIFF_D),
            ret_s_all, jnp.stack(kc_s), jnp.stack(vc_s))
```

```python
import functools
import math

import numpy as np
import jax
import jax.numpy as jnp
from jax import lax
from jax.experimental import pallas as pl
from jax.experimental.pallas import tpu as pltpu

BF = jnp.bfloat16
F32 = jnp.float32

RET_DK = 256
RET_DV = 512
RET_ROPE_BASE = 10000.0
DIFF_D = 64
DIFF_ROT = DIFF_D // 4
ROPE_THETA = 500000.0
PAGE_SIZE = 128
LN_EPS = 1e-5
N_MIXERS = 2

LANES = 128
SUBLANES = 8
PROJ_ROWS = 512
PROJ_COLS = 512
FFN_COLS = 256
NORM_ROWS = 256
FLASH_TQ = 2048
FLASH_TK = 2048
FLASH_ROWS = 256
FLASH_ROWS_DIAG = 512
RET_CHUNK = 512
RET_HEADS_PER_STEP = 2
FFN_ROWS = 512
DECODE_PAGES = 8
VMEM_LIMIT = 48 * 1024 * 1024


def _cparams(sem):
    return pltpu.CompilerParams(dimension_semantics=sem, vmem_limit_bytes=VMEM_LIMIT)


def _sigmoid(x):
    return 1.0 / (1.0 + jnp.exp(-x))


def _layer_norm(y, g, b):
    mu = jnp.mean(y, axis=-1, keepdims=True)
    d = y - mu
    var = jnp.mean(d * d, axis=-1, keepdims=True)
    return d * lax.rsqrt(var + LN_EPS) * g + b


def _row_tile(m, pref):
    return pref if m % pref == 0 else m


def _proj_plain_kernel(x_ref, w_ref, *o_refs):
    acc = jnp.dot(x_ref[...].astype(BF), w_ref[...], preferred_element_type=F32)
    for o_ref in o_refs:
        o_ref[...] = acc.astype(o_ref.dtype)


def _stacked_cols_spec(k, tn, layer, col0):
    assert col0 % tn == 0
    return pl.BlockSpec((None, k, tn), lambda i, j: (layer, 0, col0 // tn + j))


def _proj_plain(x, w, out_dtypes, *, layer, col0, n, tm, tn, name):
    m, k = x.shape
    return pl.pallas_call(
        _proj_plain_kernel,
        grid=(m // tm, n // tn),
        in_specs=[pl.BlockSpec((tm, k), lambda i, j: (i, 0)),
                  _stacked_cols_spec(k, tn, layer, col0)],
        out_specs=[pl.BlockSpec((tm, tn), lambda i, j: (i, j)) for _ in out_dtypes],
        out_shape=[jax.ShapeDtypeStruct((m, n), dt) for dt in out_dtypes],
        compiler_params=_cparams(("parallel", "parallel")),
        name=name,
    )(x, w)


def _shift_rope_kernel(x_ref, w_ref, c_ref, a_ref, b_ref, *o_refs, scale, shift):
    acc = jnp.dot(x_ref[...].astype(BF), w_ref[...], preferred_element_type=F32)
    tab_tiles = c_ref.shape[1] // LANES
    for t in range(acc.shape[1] // LANES):
        tt = t % tab_tiles
        cols = slice(tt * LANES, (tt + 1) * LANES)
        xs = acc[:, t * LANES:(t + 1) * LANES]
        up = pltpu.roll(xs, LANES - shift, axis=1)
        dn = pltpu.roll(xs, shift, axis=1)
        r = xs * c_ref[:, cols] + up * a_ref[:, cols] + dn * b_ref[:, cols]
        if scale != 1.0:
            r = r * scale
        for o_ref in o_refs:
            o_ref[:, t * LANES:(t + 1) * LANES] = r.astype(o_ref.dtype)


def _shift_rope_proj(x, w, tabs, out_dtypes, *, layer, col0, n, scale, shift, tm, tn, name):
    m, k = x.shape
    ntab = tabs[0].shape[0] // tm
    tw = min(tabs[0].shape[1], tn)
    ntab_cols = tabs[0].shape[1] // tw
    kern = functools.partial(_shift_rope_kernel, scale=scale, shift=shift)
    tab_spec = pl.BlockSpec((tm, tw), lambda i, j: (i % ntab, j % ntab_cols))
    return pl.pallas_call(
        kern,
        grid=(m // tm, n // tn),
        in_specs=[pl.BlockSpec((tm, k), lambda i, j: (i, 0)),
                  _stacked_cols_spec(k, tn, layer, col0),
                  tab_spec, tab_spec, tab_spec],
        out_specs=[pl.BlockSpec((tm, tn), lambda i, j: (i, j)) for _ in out_dtypes],
        out_shape=[jax.ShapeDtypeStruct((m, n), dt) for dt in out_dtypes],
        compiler_params=_cparams(("parallel", "parallel")),
        name=name,
    )(x, w, *tabs)


def _resident(block_shape, index_map):
    return pl.BlockSpec(block_shape, index_map, pipeline_mode=pl.Buffered(1))


def _ret_proj_kernel(x_ref, w_ref, cos_ref, sin_ref, qk_ref, v_ref, g_ref, *, heads, tn):
    xb = x_ref[...].astype(BF)
    hk, hv = heads * RET_DK, heads * RET_DV
    half = RET_DK // 2
    c = cos_ref[...]
    s = sin_ref[...]
    for h in range(2 * heads):
        acc = jnp.dot(xb, w_ref[:, h * RET_DK:(h + 1) * RET_DK], preferred_element_type=F32)
        x1 = acc[:, :half]
        x2 = acc[:, half:]
        r1 = x1 * c - x2 * s
        r2 = x2 * c + x1 * s
        if h >= heads:
            r1 = r1 * RET_DK ** -0.5
            r2 = r2 * RET_DK ** -0.5
        qk_ref[:, h * RET_DK:h * RET_DK + half] = r1.astype(qk_ref.dtype)
        qk_ref[:, h * RET_DK + half:(h + 1) * RET_DK] = r2.astype(qk_ref.dtype)
    for t in range(hv // tn):
        cols = slice(t * tn, (t + 1) * tn)
        v_ref[:, cols] = jnp.dot(xb, w_ref[:, 2 * hk + t * tn:2 * hk + (t + 1) * tn],
                                 preferred_element_type=F32).astype(v_ref.dtype)
    for t in range(hv // tn):
        cols = slice(t * tn, (t + 1) * tn)
        g_ref[:, cols] = jnp.dot(xb, w_ref[:, 2 * hk + hv + t * tn:2 * hk + hv + (t + 1) * tn],
                                 preferred_element_type=F32).astype(g_ref.dtype)


def _ret_proj(x, w_all, cos, sin, layer, *, heads, tm, name):
    m, k = x.shape
    hk, hv = heads * RET_DK, heads * RET_DV
    ntab = cos.shape[0] // tm
    kern = functools.partial(_ret_proj_kernel, heads=heads, tn=PROJ_COLS)
    tab_spec = pl.BlockSpec((tm, RET_DK // 2), lambda i: (i % ntab, 0))
    return pl.pallas_call(
        kern,
        grid=(m // tm,),
        in_specs=[pl.BlockSpec((tm, k), lambda i: (i, 0)),
                  _resident((None, k, 2 * hk + 2 * hv), lambda i: (layer, 0, 0)),
                  tab_spec, tab_spec],
        out_specs=[pl.BlockSpec((tm, 2 * hk), lambda i: (i, 0)),
                   pl.BlockSpec((tm, hv), lambda i: (i, 0)),
                   pl.BlockSpec((tm, hv), lambda i: (i, 0))],
        out_shape=[jax.ShapeDtypeStruct((m, 2 * hk), BF),
                   jax.ShapeDtypeStruct((m, hv), BF),
                   jax.ShapeDtypeStruct((m, hv), F32)],
        compiler_params=_cparams(("parallel",)),
        name=name,
    )(x, w_all, cos, sin)


def _diff_proj_kernel(*refs, hd, q_scale, shift, tn, layer, first):
    if first:
        x_ref, w_ref, c_ref, a_ref, b_ref, q_ref, kf_ref, kb_ref, vf_ref, vb_ref = refs
        for l in range(kf_ref.shape[0]):
            if l != layer:
                kf_ref[l] = jnp.zeros(kf_ref.shape[1:], F32)
                vf_ref[l] = jnp.zeros(vf_ref.shape[1:], F32)
        kf_dst, vf_dst = kf_ref.at[layer], vf_ref.at[layer]
    else:
        x_ref, w_ref, c_ref, a_ref, b_ref, _, _, q_ref, kf_dst, kb_ref, vf_dst, vb_ref = refs
    xb = x_ref[...].astype(BF)
    c = c_ref[...]
    a = a_ref[...]
    b = b_ref[...]

    def rope(acc, scale, col0, o_refs):
        for t in range(acc.shape[1] // LANES):
            xs = acc[:, t * LANES:(t + 1) * LANES]
            up = pltpu.roll(xs, LANES - shift, axis=1)
            dn = pltpu.roll(xs, shift, axis=1)
            r = xs * c + up * a + dn * b
            if scale != 1.0:
                r = r * scale
            for o_ref in o_refs:
                o_ref[:, col0 + t * LANES:col0 + (t + 1) * LANES] = r.astype(o_ref.dtype)

    for t in range(hd // tn):
        acc = jnp.dot(xb, w_ref[:, t * tn:(t + 1) * tn], preferred_element_type=F32)
        rope(acc, q_scale, t * tn, (q_ref,))
    for t in range(hd // tn):
        acc = jnp.dot(xb, w_ref[:, hd + t * tn:hd + (t + 1) * tn], preferred_element_type=F32)
        rope(acc, 1.0, t * tn, (kf_dst, kb_ref))
    for t in range(hd // tn):
        cols = slice(t * tn, (t + 1) * tn)
        acc = jnp.dot(xb, w_ref[:, 2 * hd + t * tn:2 * hd + (t + 1) * tn], preferred_element_type=F32)
        vf_dst[:, cols] = acc
        vb_ref[:, cols] = acc.astype(BF)


def _diff_proj(x, w_all, tabs, kf_all, vf_all, layer, n_layers, *, q_scale, shift, tm, name):
    m, k = x.shape
    hd = w_all.shape[2] // 3
    ntab = tabs[0].shape[0] // tm
    first = kf_all is None
    kern = functools.partial(_diff_proj_kernel, hd=hd, q_scale=q_scale, shift=shift, tn=PROJ_COLS, layer=layer,
                             first=first)
    tab_spec = pl.BlockSpec((tm, LANES), lambda i: (i % ntab, 0))
    row_spec = pl.BlockSpec((tm, hd), lambda i: (i, 0))
    in_specs = [pl.BlockSpec((tm, k), lambda i: (i, 0)),
                _resident((None, k, 3 * hd), lambda i: (layer, 0, 0)),
                tab_spec, tab_spec, tab_spec]
    args = [x, w_all, *tabs]
    if first:
        stack_spec = pl.BlockSpec((n_layers, tm, hd), lambda i: (0, i, 0))
        aliases = {}
    else:
        stack_spec = pl.BlockSpec((None, tm, hd), lambda i: (layer, i, 0))
        in_specs += [pl.BlockSpec(memory_space=pl.ANY), pl.BlockSpec(memory_space=pl.ANY)]
        args += [kf_all, vf_all]
        aliases = {5: 1, 6: 3}
    stacked = jax.ShapeDtypeStruct((n_layers, m, hd), F32)
    return pl.pallas_call(
        kern,
        grid=(m // tm,),
        in_specs=in_specs,
        out_specs=[row_spec, stack_spec, row_spec, stack_spec, row_spec],
        out_shape=[jax.ShapeDtypeStruct((m, hd), BF), stacked, jax.ShapeDtypeStruct((m, hd), BF), stacked,
                   jax.ShapeDtypeStruct((m, hd), BF)],
        input_output_aliases=aliases,
        compiler_params=_cparams(("parallel",)),
        name=name,
    )(*args)


def _mix_ffn_kernel(a_ref, wo_ref, x_ref, g1_ref, b1_ref, wg_ref, wu_ref, wd_ref, g2_ref, b2_ref,
                    o_ref, ob_ref, *, alpha, th, rb):
    wo = wo_ref[...]
    x1_rows = []
    for r in range(a_ref.shape[0] // rb):
        rows = pl.ds(r * rb, rb)
        sub = jnp.dot(a_ref[rows, :].astype(BF), wo, preferred_element_type=F32)
        x1_rows.append(_layer_norm(alpha * x_ref[rows, :] + sub, g1_ref[...], b1_ref[...]))
    x1 = jnp.concatenate(x1_rows, axis=0)
    xb = x1.astype(BF)
    hidden = wd_ref.shape[0]
    hs = []
    for c in range(hidden // th):
        cols = slice(c * th, (c + 1) * th)
        gate = jnp.dot(xb, wg_ref[:, cols], preferred_element_type=F32)
        up = jnp.dot(xb, wu_ref[:, cols], preferred_element_type=F32)
        hs.append((gate * _sigmoid(gate) * up).astype(BF))
    h = jnp.concatenate(hs, axis=1)
    sub = jnp.dot(h, wd_ref[...], preferred_element_type=F32)
    y = _layer_norm(alpha * x1 + sub, g2_ref[...], b2_ref[...])
    o_ref[...] = y
    ob_ref[...] = y.astype(BF)


def _mix_ffn(a, w_o, mixer, x, g1, b1, w_in, w_down, layer, g2, b2, *, alpha, tm, th, name):
    m, ka = a.shape
    d = x.shape[1]
    hidden = w_down.shape[1]
    kern = functools.partial(_mix_ffn_kernel, alpha=alpha, th=th, rb=min(tm, NORM_ROWS))
    vec = pl.BlockSpec((1, d), lambda i: (0, 0))
    return pl.pallas_call(
        kern,
        grid=(m // tm,),
        in_specs=[pl.BlockSpec((tm, ka), lambda i: (i, 0)),
                  _resident((None, ka, d), lambda i: (mixer, 0, 0)),
                  pl.BlockSpec((tm, d), lambda i: (i, 0)),
                  vec, vec,
                  _resident((None, d, hidden), lambda i: (layer, 0, 0)),
                  _resident((None, d, hidden), lambda i: (layer, 0, 1)),
                  _resident((None, hidden, d), lambda i: (layer, 0, 0)),
                  vec, vec],
        out_specs=[pl.BlockSpec((tm, d), lambda i: (i, 0)),
                   pl.BlockSpec((tm, d), lambda i: (i, 0))],
        out_shape=[jax.ShapeDtypeStruct((m, d), F32), jax.ShapeDtypeStruct((m, d), BF)],
        compiler_params=_cparams(("parallel",)),
        name=name,
    )(a, w_o, x, g1, b1, w_in, w_in, w_down, g2, b2)


def _ret_chunk_kernel(lg_ref, q_ref, k_ref, v_ref, g_ref, gn_ref, o_ref, st_ref, s_ref, decay_ref, *, chunk, hps):
    c = pl.program_id(2)
    half = RET_DK // 2

    @pl.when(c == 0)
    def _():
        ii = lax.broadcasted_iota(jnp.int32, (chunk, chunk), 0)
        jj = lax.broadcasted_iota(jnp.int32, (chunk, chunk), 1)
        rel = (ii - jj).astype(F32)
        for hh in range(hps):
            s_ref[hh] = jnp.zeros(s_ref.shape[1:], F32)
            decay_ref[hh] = jnp.where(rel >= 0, jnp.exp(lg_ref[hh] * jnp.maximum(rel, 0.0)), 0.0)

    idx = lax.broadcasted_iota(jnp.int32, (chunk, 1), 0).astype(F32)
    for hh in range(hps):
        lg = lg_ref[hh][:, :1]
        q_decay = jnp.exp(lg * (idx + 1.0))
        k_decay = jnp.exp(lg * (chunk - 1.0 - idx))
        chunk_decay = jnp.exp(lg * float(chunk))

        q = q_ref[:, hh * RET_DK:(hh + 1) * RET_DK]
        k = k_ref[:, hh * RET_DK:(hh + 1) * RET_DK]
        v = v_ref[:, hh * RET_DV:(hh + 1) * RET_DV]
        s_old = s_ref[hh]
        scores = lax.dot_general(q, k, (((1,), (1,)), ((), ())), preferred_element_type=F32) * decay_ref[hh]
        inner = jnp.dot(scores.astype(BF), v, preferred_element_type=F32)
        cross = jnp.dot(q, s_old.astype(BF), preferred_element_type=F32) * q_decay
        kd_t = (k.astype(F32) * k_decay).T.astype(BF)
        s_ref[hh] = chunk_decay * s_old + jnp.dot(kd_t, v, preferred_element_type=F32)

        o = inner + cross
        mu = jnp.mean(o, axis=-1, keepdims=True)
        d = o - mu
        var = jnp.mean(d * d, axis=-1, keepdims=True)
        on = d * lax.rsqrt(var + LN_EPS) * gn_ref[:, hh * RET_DV:(hh + 1) * RET_DV]
        gt = g_ref[:, hh * RET_DV:(hh + 1) * RET_DV].astype(F32)
        o_ref[:, hh * RET_DV:(hh + 1) * RET_DV] = (gt * _sigmoid(gt) * on).astype(o_ref.dtype)

    @pl.when(c == pl.num_programs(2) - 1)
    def _():
        for hh in range(hps):
            st_ref[0, hh, :, :RET_DV] = s_ref[hh, :half, :]
            st_ref[0, hh, :, RET_DV:] = s_ref[hh, half:, :]


def _ret_chunk(lg_tab, qk, v, g, gn_g, *, batch, heads, chunk, name):
    m = qk.shape[0]
    t = m // batch
    nc = t // chunk
    hps = RET_HEADS_PER_STEP if heads % RET_HEADS_PER_STEP == 0 else 1
    ng = heads // hps
    kern = functools.partial(_ret_chunk_kernel, chunk=chunk, hps=hps)
    return pl.pallas_call(
        kern,
        grid=(batch, ng, nc),
        in_specs=[pl.BlockSpec((hps, 1, chunk), lambda b, h, c: (h, 0, 0)),
                  pl.BlockSpec((chunk, hps * RET_DK), lambda b, h, c: (b * nc + c, h)),
                  pl.BlockSpec((chunk, hps * RET_DK), lambda b, h, c: (b * nc + c, ng + h)),
                  pl.BlockSpec((chunk, hps * RET_DV), lambda b, h, c: (b * nc + c, h)),
                  pl.BlockSpec((chunk, hps * RET_DV), lambda b, h, c: (b * nc + c, h)),
                  pl.BlockSpec((1, hps * RET_DV), lambda b, h, c: (0, h))],
        out_specs=[pl.BlockSpec((chunk, hps * RET_DV), lambda b, h, c: (b * nc + c, h)),
                   pl.BlockSpec((1, hps, RET_DK // 2, 2 * RET_DV), lambda b, h, c: (b, h, 0, 0))],
        out_shape=[jax.ShapeDtypeStruct((m, heads * RET_DV), BF),
                   jax.ShapeDtypeStruct((batch, heads, RET_DK // 2, 2 * RET_DV), F32)],
        scratch_shapes=[pltpu.VMEM((hps, RET_DK, RET_DV), F32), pltpu.VMEM((hps, chunk, chunk), F32)],
        compiler_params=_cparams(("parallel", "parallel", "arbitrary")),
        name=name,
    )(lg_tab, qk, qk, v, g, gn_g)


def _ret_step_kernel(lg_ref, q_ref, k_ref, v_ref, g_ref, gn_ref, s0_ref, o_ref, *, heads):
    qrow = q_ref[0]
    krow = k_ref[0]
    vrow = v_ref[0]
    grow = g_ref[0]
    for h in range(heads):
        gamma = jnp.exp(lg_ref[h][:, :1])
        qh = qrow[:, h * RET_DK:(h + 1) * RET_DK]
        kh = krow[:, h * RET_DK:(h + 1) * RET_DK]
        vh = vrow[:, h * RET_DV:(h + 1) * RET_DV]
        s0 = s0_ref[0, 0, h]
        q8 = jnp.broadcast_to(qh, (SUBLANES, RET_DK)).astype(BF)
        cross = jnp.dot(q8, s0.astype(BF), preferred_element_type=F32)[:1]
        qk = jnp.sum(qh * kh, axis=-1, keepdims=True)
        o = qk * vh + gamma * cross
        mu = jnp.mean(o, axis=-1, keepdims=True)
        d = o - mu
        var = jnp.mean(d * d, axis=-1, keepdims=True)
        on = d * lax.rsqrt(var + LN_EPS) * gn_ref[:, h * RET_DV:(h + 1) * RET_DV]
        gt = grow[:, h * RET_DV:(h + 1) * RET_DV]
        o_ref[0, :, h * RET_DV:(h + 1) * RET_DV] = (gt * _sigmoid(gt) * on).astype(o_ref.dtype)


def _ret_step(lg_tab, qk, v, g, gn_g, state_all, layer, *, heads, name):
    bs = qk.shape[0]
    hk = heads * RET_DK
    hv = heads * RET_DV
    q3 = qk.reshape(bs, 1, 2 * hk)
    v3 = v.reshape(bs, 1, hv)
    g3 = g.reshape(bs, 1, hv)
    kern = functools.partial(_ret_step_kernel, heads=heads)
    out = pl.pallas_call(
        kern,
        grid=(bs,),
        in_specs=[pl.BlockSpec(lg_tab.shape, lambda b: (0, 0, 0)),
                  pl.BlockSpec((1, 1, hk), lambda b: (b, 0, 0)),
                  pl.BlockSpec((1, 1, hk), lambda b: (b, 0, 1)),
                  pl.BlockSpec((1, 1, hv), lambda b: (b, 0, 0)),
                  pl.BlockSpec((1, 1, hv), lambda b: (b, 0, 0)),
                  pl.BlockSpec((1, hv), lambda b: (0, 0)),
                  pl.BlockSpec((1, 1, heads, RET_DK, RET_DV), lambda b: (layer, b, 0, 0, 0))],
        out_specs=pl.BlockSpec((1, 1, hv), lambda b: (b, 0, 0)),
        out_shape=jax.ShapeDtypeStruct((bs, 1, hv), BF),
        compiler_params=_cparams(("parallel",)),
        name=name,
    )(lg_tab, q3, q3, v3, g3, gn_g, state_all)
    return out.reshape(bs, hv)


def _ret_new_state_kernel(lg_ref, k_ref, v_ref, s0_ref, st_ref, *, heads):
    krow = k_ref[0, 0]
    vrow = v_ref[0, 0]
    for h in range(heads):
        gamma = jnp.exp(lg_ref[h][:, :1])
        kh = krow[:, h * RET_DK:(h + 1) * RET_DK]
        vh = vrow[:, h * RET_DV:(h + 1) * RET_DV]
        k_col = jnp.broadcast_to(kh, (SUBLANES, RET_DK)).T[:, :1]
        st_ref[0, 0, h] = gamma * s0_ref[0, 0, h] + k_col * vh


def _ret_new_state(lg_tab, k_all, v_all, state_all, *, heads, name):
    n_layers, bs = state_all.shape[:2]
    hk = heads * RET_DK
    hv = heads * RET_DV
    kern = functools.partial(_ret_new_state_kernel, heads=heads)
    st_spec = pl.BlockSpec((1, 1, heads, RET_DK, RET_DV), lambda l, b: (l, b, 0, 0, 0))
    return pl.pallas_call(
        kern,
        grid=(n_layers, bs),
        in_specs=[pl.BlockSpec(lg_tab.shape, lambda l, b: (0, 0, 0)),
                  pl.BlockSpec((1, 1, 1, hk), lambda l, b: (l, b, 0, 0)),
                  pl.BlockSpec((1, 1, 1, hv), lambda l, b: (l, b, 0, 0)),
                  st_spec],
        out_specs=st_spec,
        out_shape=jax.ShapeDtypeStruct(state_all.shape, F32),
        compiler_params=_cparams(("parallel", "parallel")),
        name=name,
    )(lg_tab, k_all, v_all, state_all)


def _diff_lambda(lq1_ref, lk1_ref, lq2_ref, lk2_ref, lam_init):
    a = jnp.sum(lq1_ref[...] * lk1_ref[...], axis=-1, keepdims=True)
    b = jnp.sum(lq2_ref[...] * lk2_ref[...], axis=-1, keepdims=True)
    return jnp.exp(a) - jnp.exp(b) + lam_init


def _diff_flash_kernel(qi_ref, kj_ref, q_ref, k_ref, v_ref, lq1_ref, lk1_ref, lq2_ref, lk2_ref, sg_ref,
                       o_ref, q1_ref, q2_ref, m1_ref, m2_ref, l1_ref, l2_ref, a1_ref, a2_ref,
                       *, tq, tk, rb_full, rb_diag, lam_init):
    step = pl.program_id(2)
    qi = qi_ref[step]
    kj = kj_ref[step]
    last_kj = ((qi + 1) * tq - 1) // tk
    comps = ((q1_ref, m1_ref, l1_ref, a1_ref), (q2_ref, m2_ref, l2_ref, a2_ref))

    @pl.when(kj == 0)
    def _():
        q = q_ref[...]
        lane = lax.broadcasted_iota(jnp.int32, q.shape, 1)
        zero = jnp.zeros_like(q)
        q1_ref[...] = jnp.where(lane < DIFF_D, q, zero)
        q2_ref[...] = jnp.where(lane >= DIFF_D, q, zero)
        for _, m_ref, l_ref, a_ref in comps:
            m_ref[...] = jnp.full_like(m_ref, -jnp.inf)
            l_ref[...] = jnp.zeros_like(l_ref)
            a_ref[...] = jnp.zeros_like(a_ref)

    def update(masked):
        rb = rb_diag if masked else rb_full
        k = k_ref[...]
        v = v_ref[...]
        v_ext = jnp.concatenate([v, jnp.ones_like(v)], axis=1)
        tri = (lax.broadcasted_iota(jnp.int32, (rb, rb), 1) <= lax.broadcasted_iota(jnp.int32, (rb, rb), 0))
        def chain(qc_ref, m_ref, l_ref, a_ref, r):
            rows = pl.ds(r * rb, rb)
            q_r = qc_ref[rows, :]
            nt = (((1,), (1,)), ((), ()))
            if masked and tq == tk:
                lo = r * rb
                parts = []
                if lo > 0:
                    parts.append((lax.dot_general(q_r, k[:lo], nt, preferred_element_type=F32), v_ext[:lo]))
                s_d = lax.dot_general(q_r, k[lo:lo + rb], nt, preferred_element_type=F32)
                parts.append((jnp.where(tri, s_d, -jnp.inf), v_ext[lo:lo + rb]))
            else:
                s = lax.dot_general(q_r, k, nt, preferred_element_type=F32)
                if masked:
                    row = lax.broadcasted_iota(jnp.int32, (rb, tk), 0) + (qi * tq + r * rb)
                    col = lax.broadcasted_iota(jnp.int32, (rb, tk), 1) + kj * tk
                    s = jnp.where(col <= row, s, -jnp.inf)
                parts = [(s, v_ext)]
            m_prev = m_ref[rows, :]
            m_new = m_prev
            for s, _ in parts:
                m_new = jnp.maximum(m_new, jnp.max(s, axis=-1, keepdims=True))
            alpha = jnp.exp2(m_prev - m_new)
            pv = None
            for s, vx in parts:
                p = jnp.exp2(s - jnp.tile(m_new, (1, s.shape[1] // LANES)))
                d = jnp.dot(p.astype(BF), vx, preferred_element_type=F32)
                pv = d if pv is None else pv + d
            a_ref[rows, :] = alpha * a_ref[rows, :] + pv[:, :LANES]
            l_ref[rows, :] = alpha * l_ref[rows, :] + pv[:, LANES:]
            m_ref[rows, :] = m_new

        for qc_ref, m_ref, l_ref, a_ref in comps:
            for r in range(tq // rb):
                chain(qc_ref, m_ref, l_ref, a_ref, r)


    straddles = (kj + 1) * tk - 1 > qi * tq

    @pl.when(straddles)
    def _():
        update(True)

    @pl.when(jnp.logical_not(straddles))
    def _():
        update(False)

    @pl.when(kj == last_kj)
    def _():
        lam = _diff_lambda(lq1_ref, lk1_ref, lq2_ref, lk2_ref, lam_init)
        o = a1_ref[...] / l1_ref[...] - lam * (a2_ref[...] / l2_ref[...])
        o = o * lax.rsqrt(jnp.mean(o * o, axis=-1, keepdims=True) + LN_EPS)
        o_ref[...] = (o * sg_ref[...] * (1.0 - lam_init)).astype(o_ref.dtype)


def _diff_flash(q, k, v, lq1, lk1, lq2, lk2, subln_g, *, batch, heads, tq, tk, rb_full, rb_diag, lam_init, name):
    m = q.shape[0]
    t = m // batch
    nq, nk = t // tq, t // tk
    qi_list, kj_list = [], []
    for i in range(nq):
        for j in range(((i + 1) * tq - 1) // tk + 1):
            qi_list.append(i)
            kj_list.append(j)
    qi_tab = jnp.asarray(np.array(qi_list, np.int32))
    kj_tab = jnp.asarray(np.array(kj_list, np.int32))
    hd = 2 * DIFF_D
    assert hd == LANES
    kern = functools.partial(_diff_flash_kernel, tq=tq, tk=tk, rb_full=rb_full, rb_diag=rb_diag, lam_init=lam_init)
    vec_spec = pl.BlockSpec((1, DIFF_D), lambda b, h, s, qi, kj: (0, 0))
    grid_spec = pltpu.PrefetchScalarGridSpec(
        num_scalar_prefetch=2,
        grid=(batch, heads, len(qi_list)),
        in_specs=[pl.BlockSpec((tq, hd), lambda b, h, s, qi, kj: (b * nq + qi[s], h)),
                  pl.BlockSpec((tk, hd), lambda b, h, s, qi, kj: (b * nk + kj[s], h)),
                  pl.BlockSpec((tk, hd), lambda b, h, s, qi, kj: (b * nk + kj[s], h)),
                  vec_spec, vec_spec, vec_spec, vec_spec,
                  pl.BlockSpec((1, hd), lambda b, h, s, qi, kj: (0, h))],
        out_specs=pl.BlockSpec((tq, hd), lambda b, h, s, qi, kj: (b * nq + qi[s], h)),
        scratch_shapes=[pltpu.VMEM((tq, hd), BF), pltpu.VMEM((tq, hd), BF)]
                       + [pltpu.VMEM((tq, hd), F32) for _ in range(6)],
    )
    return pl.pallas_call(
        kern,
        grid_spec=grid_spec,
        out_shape=jax.ShapeDtypeStruct((m, heads * hd), BF),
        compiler_params=_cparams(("parallel", "parallel", "arbitrary")),
        name=name,
    )(qi_tab, kj_tab, q, k, v, lq1, lk1, lq2, lk2, subln_g)


def _diff_decode_kernel(pt_ref, q_ref, kn_ref, vn_ref, lq1_ref, lk1_ref, lq2_ref, lk2_ref, sg_ref, *rest,
                        heads, n_fetch, lam_init):
    del pt_ref
    k_refs = rest[:n_fetch]
    v_refs = rest[n_fetch:2 * n_fetch]
    o_ref, m_ref, l_ref, acc_ref = rest[2 * n_fetch:]
    p_idx = pl.program_id(1)
    rows = 2 * heads

    @pl.when(p_idx == 0)
    def _():
        m_ref[...] = jnp.full_like(m_ref, -jnp.inf)
        l_ref[...] = jnp.zeros_like(l_ref)
        acc_ref[...] = jnp.zeros_like(acc_ref)

    q8 = q_ref[0]
    lane = lax.broadcasted_iota(jnp.int32, q8.shape, 1)
    q16 = jnp.concatenate([jnp.where(lane < DIFF_D, q8, 0.0),
                           jnp.where(lane >= DIFF_D, q8, 0.0)], axis=0)
    q16b = q16.astype(BF)

    ncol = k_refs[0].shape[1]
    r_head = lax.broadcasted_iota(jnp.int32, (rows, ncol), 0) % heads
    c_head = lax.broadcasted_iota(jnp.int32, (rows, ncol), 1) % heads
    same_head = r_head == c_head
    s_pages = []
    for k_ref in k_refs:
        s = lax.dot_general(q16b, k_ref[0].astype(BF), (((1,), (1,)), ((), ())),
                            preferred_element_type=F32)
        s_pages.append(jnp.where(same_head, s, -jnp.inf))
    m_prev = m_ref[...]
    m_new = m_prev
    for s in s_pages:
        m_new = jnp.maximum(m_new, jnp.max(s, axis=-1, keepdims=True))
    alpha = jnp.exp(m_prev - m_new)
    l_new = alpha * l_ref[...]
    acc = alpha * acc_ref[...]
    for s, v_ref in zip(s_pages, v_refs):
        p = jnp.exp(s - m_new)
        l_new = l_new + jnp.sum(p, axis=-1, keepdims=True)
        acc = acc + jnp.dot(p.astype(BF), v_ref[0].astype(BF), preferred_element_type=F32)
    l_ref[...] = l_new
    acc_ref[...] = acc
    m_ref[...] = m_new

    @pl.when(p_idx == pl.num_programs(1) - 1)
    def _():
        kn = kn_ref[0]
        vn = vn_ref[0]
        kn2 = jnp.concatenate([kn, kn], axis=0)
        vn2 = jnp.concatenate([vn, vn], axis=0)
        s_n = jnp.sum(q16 * kn2, axis=-1, keepdims=True)
        m_fin = jnp.maximum(m_new, s_n)
        beta = jnp.exp(m_new - m_fin)
        p_n = jnp.exp(s_n - m_fin)
        l_fin = beta * l_new + p_n
        acc_fin = beta * acc + p_n * vn2
        lam = _diff_lambda(lq1_ref, lk1_ref, lq2_ref, lk2_ref, lam_init)
        o = acc_fin[:heads] / l_fin[:heads] - lam * (acc_fin[heads:] / l_fin[heads:])
        o = o * lax.rsqrt(jnp.mean(o * o, axis=-1, keepdims=True) + LN_EPS)
        o_ref[0] = (o * sg_ref[...] * (1.0 - lam_init)).astype(o_ref.dtype)


def _diff_decode(page_table, q, kn, vn, lq1, lk1, lq2, lk2, subln_g, cache_k, cache_v, layer, *,
                 heads, lam_init, name):
    bs, n_pages = page_table.shape
    hd = 2 * DIFF_D
    n_layers, n_phys = cache_k.shape[:2]
    ck = cache_k.reshape(n_layers * n_phys, PAGE_SIZE * heads, hd)
    cv = cache_v.reshape(n_layers * n_phys, PAGE_SIZE * heads, hd)
    q3 = q.reshape(bs, heads, hd)
    kn3 = kn.reshape(bs, heads, hd)
    vn3 = vn.reshape(bs, heads, hd)
    sg = subln_g.reshape(heads, hd)
    pt = page_table.reshape(-1)
    n_fetch = DECODE_PAGES if n_pages % DECODE_PAGES == 0 else 1
    base = layer * n_phys
    kern = functools.partial(_diff_decode_kernel, heads=heads, n_fetch=n_fetch, lam_init=lam_init)
    row_spec = pl.BlockSpec((1, heads, hd), lambda b, p, pt: (b, 0, 0))
    vec_spec = pl.BlockSpec((1, DIFF_D), lambda b, p, pt: (0, 0))

    def page_spec(r):
        return pl.BlockSpec((1, PAGE_SIZE * heads, hd),
                            lambda b, p, pt: (base + pt[b * n_pages + p * n_fetch + r], 0, 0))

    page_specs = [page_spec(r) for r in range(n_fetch)]
    grid_spec = pltpu.PrefetchScalarGridSpec(
        num_scalar_prefetch=1,
        grid=(bs, n_pages // n_fetch),
        in_specs=[row_spec, row_spec, row_spec, vec_spec, vec_spec, vec_spec, vec_spec,
                  pl.BlockSpec((heads, hd), lambda b, p, pt: (0, 0))] + page_specs + page_specs,
        out_specs=pl.BlockSpec((1, heads, hd), lambda b, p, pt: (b, 0, 0)),
        scratch_shapes=[pltpu.VMEM((2 * heads, 1), F32),
                        pltpu.VMEM((2 * heads, 1), F32),
                        pltpu.VMEM((2 * heads, hd), F32)],
    )
    out = pl.pallas_call(
        kern,
        grid_spec=grid_spec,
        out_shape=jax.ShapeDtypeStruct((bs, heads, hd), BF),
        compiler_params=_cparams(("parallel", "arbitrary")),
        name=name,
    )(pt, q3, kn3, vn3, lq1, lk1, lq2, lk2, sg, *([ck] * n_fetch), *([cv] * n_fetch))
    return out.reshape(bs, heads * hd)


def _ret_tables(pos):
    inv = 1.0 / (RET_ROPE_BASE ** jnp.linspace(0.0, 1.0, RET_DK // 2, dtype=F32))
    ang = pos.astype(F32)[:, None] * inv[None, :]
    return jnp.cos(ang), jnp.sin(ang)


def _ret_tables_interleaved(pos, heads):
    cos, sin = _ret_tables(pos)
    zero = jnp.zeros_like(sin)
    c = jnp.stack([cos, cos], axis=-1).reshape(pos.shape[0], RET_DK)
    a = jnp.stack([-sin, zero], axis=-1).reshape(pos.shape[0], RET_DK)
    b = jnp.stack([zero, sin], axis=-1).reshape(pos.shape[0], RET_DK)
    k_scale = RET_DK ** -0.5
    return tuple(jnp.concatenate([jnp.tile(tab, (1, heads)), jnp.tile(tab * k_scale, (1, heads))], axis=1)
                 for tab in (c, a, b))


def _diff_tables(pos):
    half = DIFF_ROT // 2
    inv = 1.0 / (ROPE_THETA ** (jnp.arange(half, dtype=F32) * 2.0 / DIFF_ROT))
    ang = pos.astype(F32)[:, None] * inv[None, :]
    cos, sin = jnp.cos(ang), jnp.sin(ang)
    n = pos.shape[0]
    ones = jnp.ones((n, DIFF_D - DIFF_ROT), F32)
    zeros_rest = jnp.zeros((n, DIFF_D - DIFF_ROT), F32)
    zeros_h = jnp.zeros((n, half), F32)
    c = jnp.concatenate([cos, cos, ones], axis=1)
    a = jnp.concatenate([-sin, zeros_h, zeros_rest], axis=1)
    b = jnp.concatenate([zeros_h, sin, zeros_rest], axis=1)
    rep = LANES // DIFF_D
    return tuple(jnp.tile(tab, (1, rep)) for tab in (c, a, b))


def _ret_prompt_weights(w_in, heads):
    hk = heads * RET_DK
    perm = np.concatenate([np.arange(0, RET_DK, 2), np.arange(1, RET_DK, 2)])
    qk_cols = np.concatenate([h * RET_DK + perm for h in range(2 * heads)])
    w_qk = w_in[:, :2 * hk][:, qk_cols]
    return jnp.concatenate([w_qk, w_in[:, 2 * hk:]], axis=1).astype(BF)[None]


def kernel(x_prompt, x_sample, state_ret, cache_k, cache_v, page_table, ret_w_in, ret_gn_g, ret_w_o, diff_w_in, diff_lq1, diff_lk1, diff_lq2, diff_lk2, diff_subln_g, diff_w_o, ffn_w_in, ffn_w_down, ln1_g, ln1_b, ln2_g, ln2_b):
    bp, tp, d_model = x_prompt.shape
    bs, ts, _ = x_sample.shape
    assert ts == 1
    depth = ffn_w_in.shape[0]
    ret_heads = ret_w_o.shape[1] // RET_DV
    diff_heads = diff_w_o.shape[1] // (2 * DIFF_D)
    hd = diff_heads * 2 * DIFF_D
    past_len = page_table.shape[1] * PAGE_SIZE
    alpha = (2.0 * depth) ** 0.25
    mp = bp * tp

    pos_p = jnp.arange(tp)
    pos_s = jnp.broadcast_to(past_len + jnp.arange(ts), (bs,))
    ret_tab_p = _ret_tables(pos_p)
    ret_tab_s = _ret_tables_interleaved(pos_s, ret_heads)
    diff_tab_p = _diff_tables(pos_p)
    diff_tab_s = _diff_tables(pos_s)

    ret_chunk = RET_CHUNK if tp % RET_CHUNK == 0 else tp
    lg = jnp.log(1.0 - 2.0 ** (-5.0 - jnp.arange(ret_heads, dtype=F32)))
    lg_tab = jnp.broadcast_to(lg[:, None, None], (ret_heads, 1, max(ret_chunk, LANES)))

    tm_p = _row_tile(mp, PROJ_ROWS)
    tm_s = bs

    xp = x_prompt.reshape(mp, d_model)
    xs = x_sample.reshape(bs * ts, d_model)
    xp_b, xs_b = xp, xs

    n_diff = diff_w_in.shape[0]
    ret_p, kc_s, vc_s, ret_k_s, ret_v_s = [], [], [], [], []
    kc_p_all = vc_p_all = None

    hk, hv = ret_heads * RET_DK, ret_heads * RET_DV
    ret_w_s = ret_w_in[:, :, :2 * hk].astype(BF)
    ret_wo = ret_w_o.astype(BF)
    diff_w = diff_w_in.astype(BF)
    diff_wo = diff_w_o.astype(BF)
    ffn_wi = ffn_w_in.astype(BF)
    ffn_wd = ffn_w_down.astype(BF)

    for i in range(depth):
        j = i // N_MIXERS
        g1, b1 = ln1_g[i][None, :], ln1_b[i][None, :]
        g2, b2 = ln2_g[i][None, :], ln2_b[i][None, :]
        if i % N_MIXERS == 0:
            w_o = ret_wo
            gn = ret_gn_g[j][None, :]
            ret_w_p = _ret_prompt_weights(ret_w_in[j], ret_heads)
            qk, v, g = _ret_proj(xp_b, ret_w_p, *ret_tab_p, 0, heads=ret_heads, tm=tm_p, name=f"ret{j}_proj_p")
            mix_p, st_p = _ret_chunk(lg_tab, qk, v, g, gn, batch=bp, heads=ret_heads, chunk=ret_chunk,
                                     name=f"ret{j}_chunk_p")
            ret_p.append(st_p.reshape(bp, ret_heads, RET_DK, RET_DV))
            (qk_s,) = _shift_rope_proj(xs_b, ret_w_s, ret_tab_s, [F32], layer=j, col0=0, n=2 * hk, scale=1.0,
                                       shift=1, tm=tm_s, tn=PROJ_COLS, name=f"ret{j}_qk_s")
            (v_s,) = _proj_plain(xs_b, ret_w_p, [F32], layer=0, col0=2 * hk, n=hv, tm=tm_s, tn=PROJ_COLS,
                                 name=f"ret{j}_v_s")
            (g_s,) = _proj_plain(xs_b, ret_w_p, [F32], layer=0, col0=2 * hk + hv, n=hv, tm=tm_s, tn=PROJ_COLS,
                                 name=f"ret{j}_g_s")
            mix_s = _ret_step(lg_tab, qk_s, v_s, g_s, gn, state_ret, j, heads=ret_heads, name=f"ret{j}_step_s")
            ret_k_s.append(qk_s[:, hk:])
            ret_v_s.append(v_s)
        else:
            lam_init = 0.8 - 0.6 * math.exp(-0.3 * (i + 1))
            w_o = diff_wo
            lam_args = (diff_lq1[j][None, :], diff_lk1[j][None, :], diff_lq2[j][None, :], diff_lk2[j][None, :])
            sg = diff_subln_g[j][None, :]
            rot = DIFF_ROT // 2
            q, kc_p_all, k_b, vc_p_all, v_b = _diff_proj(xp_b, diff_w, diff_tab_p, kc_p_all, vc_p_all, j, n_diff,
                                                         q_scale=DIFF_D ** -0.5 * math.log2(math.e), shift=rot,
                                                         tm=tm_p, name=f"diff{j}_proj_p")
            tq = FLASH_TQ if tp % FLASH_TQ == 0 else tp
            tk = FLASH_TK if tp % FLASH_TK == 0 else tp
            mix_p = _diff_flash(q, k_b, v_b, *lam_args, sg, batch=bp, heads=diff_heads, tq=tq, tk=tk,
                                rb_full=min(FLASH_ROWS, tq), rb_diag=min(FLASH_ROWS_DIAG, tq),
                                lam_init=lam_init, name=f"diff{j}_flash_p")
            (q_s,) = _shift_rope_proj(xs_b, diff_w, diff_tab_s, [F32], layer=j, col0=0, n=hd, scale=DIFF_D ** -0.5,
                                      shift=rot, tm=tm_s, tn=PROJ_COLS, name=f"diff{j}_q_s")
            (kn,) = _shift_rope_proj(xs_b, diff_w, diff_tab_s, [F32], layer=j, col0=hd, n=hd, scale=1.0,
                                     shift=rot, tm=tm_s, tn=PROJ_COLS, name=f"diff{j}_k_s")
            (vn,) = _proj_plain(xs_b, diff_w, [F32], layer=j, col0=2 * hd, n=hd, tm=tm_s, tn=PROJ_COLS,
                                name=f"diff{j}_v_s")
            mix_s = _diff_decode(page_table, q_s, kn, vn, *lam_args, sg, cache_k, cache_v, j,
                                 heads=diff_heads, lam_init=lam_init, name=f"diff{j}_decode_s")
            kc_s.append(kn.reshape(bs, ts, diff_heads, 2 * DIFF_D))
            vc_s.append(vn.reshape(bs, ts, diff_heads, 2 * DIFF_D))

        xp, xp_b = _mix_ffn(mix_p, w_o, j, xp, g1, b1, ffn_wi, ffn_wd, i, g2, b2, alpha=alpha,
                            tm=_row_tile(mp, FFN_ROWS), th=FFN_COLS, name=f"l{i}_mix_ffn_p")
        xs, xs_b = _mix_ffn(mix_s, w_o, j, xs, g1, b1, ffn_wi, ffn_wd, i, g2, b2, alpha=alpha, tm=tm_s,
                            th=FFN_COLS, name=f"l{i}_mix_ffn_s")

    ret_s_all = _ret_new_state(lg_tab, jnp.stack(ret_k_s)[:, :, None, :], jnp.stack(ret_v_s)[:, :, None, :],
                               state_ret, heads=ret_heads, name="ret_new_state_s")
    return (xp.reshape(bp, tp, d_model), xs.reshape(bs, ts, d_model),
            jnp.stack(ret_p),
            kc_p_all.reshape(n_diff, bp, tp, diff_heads, 2 * DIFF_D),
            vc_p_all.reshape(n_diff, bp, tp, diff_heads, 2 * DIFF_D),
            ret_s_all, jnp.stack(kc_s), jnp.stack(vc_s))
```
